```python
import math
import jax, jax.numpy as jnp
from jax import lax
import numpy as np

D_MODEL = 1024
BATCH = 16
SEQ = 2048
DEPTH = 4

N_MIXERS = 3
N_A = (DEPTH + 2) // 3
N_B = (DEPTH + 1) // 3
N_C = DEPTH // 3
EPS = 1e-6
CONV_W = 4
N_MEM = 256

M_D_INNER = 2 * D_MODEL
M_HEAD_DIM = 64
M_HEADS = M_D_INNER // M_HEAD_DIM
M_GROUPS = 8
M_STATE = 128
M_CONV_DIM = M_D_INNER + 2 * M_GROUPS * M_STATE
M_IN = M_D_INNER + M_CONV_DIM + M_HEADS
SSD_CHUNK = 64
DT_MIN = 1e-3
DT_MAX = 1e-1

H_EXPAND = 128
H_HEADS = D_MODEL // H_EXPAND
H_DV = D_MODEL // H_HEADS
HGRN_CHUNK = 32

G_HEAD_DIM = 128
G_QK_HEADS = D_MODEL // G_HEAD_DIM
G_V_HEADS = 2 * G_QK_HEADS
G_KEY_DIM = G_QK_HEADS * G_HEAD_DIM
G_VAL_DIM = G_V_HEADS * G_HEAD_DIM
G_CONV_DIM = 2 * G_KEY_DIM + G_VAL_DIM
G_IN = G_CONV_DIM + G_VAL_DIM + 2 * G_V_HEADS
GDN_CHUNK = 64

X_HEADS = 4
X_HEAD_DIM = D_MODEL // X_HEADS

D_FF = 2816
FFN_CONV_W = 3

kernel_name = 'hybrid_ssd_hgrn2_gdn_memxattn_block'


def rmsnorm(x, w):
    x32 = x.astype(jnp.float32)
    y = x32 * lax.rsqrt(jnp.mean(x32 * x32, axis=-1, keepdims=True) + EPS)
    return (y * w.astype(jnp.float32)).astype(x.dtype)


def l2norm(x):
    x32 = x.astype(jnp.float32)
    return x32 * lax.rsqrt(jnp.sum(x32 * x32, axis=-1, keepdims=True) + EPS)


def causal_dwconv(x, w):
    width, ch = w.shape
    return lax.conv_general_dilated(x, w[:, None, :].astype(x.dtype), window_strides=(1,),
                                    padding=[(width - 1, 0)],
                                    dimension_numbers=('NWC', 'WIO', 'NWC'),
                                    feature_group_count=ch)


def ssd_chunk(x, dt, a, bm, cm, chunk):
    bsz, seq, nh, p = x.shape
    ng, ns = bm.shape[2], bm.shape[3]
    r = nh // ng
    nc = seq // chunk
    f32 = jnp.float32
    xc = (x * dt[..., None]).astype(f32).reshape(bsz, nc, chunk, ng, r, p)
    acum = jnp.cumsum((dt * a).astype(f32).reshape(bsz, nc, chunk, ng, r), axis=2)
    bc = bm.astype(f32).reshape(bsz, nc, chunk, ng, ns)
    cc = cm.astype(f32).reshape(bsz, nc, chunk, ng, ns)
    causal = jnp.tril(jnp.ones((chunk, chunk), dtype=bool))[:, :, None, None]
    decay = jnp.exp(jnp.where(causal, acum[:, :, :, None] - acum[:, :, None, :], -jnp.inf))
    cb = jnp.einsum('bnlgk,bnsgk->bnlsg', cc, bc)
    y_diag = jnp.einsum('bnlsgr,bnsgrp->bnlgrp', cb[..., None] * decay, xc)

    def step(state, inp):
        c_, b_, x_, ac_ = inp
        y = jnp.einsum('blgk,bgrpk->blgrp', c_, state) * jnp.exp(ac_)[..., None]
        last = ac_[:, -1]
        ds = jnp.einsum('bsgk,bsgrp->bgrpk', b_, x_ * jnp.exp(last[:, None] - ac_)[..., None])
        state = state * jnp.exp(last)[..., None, None] + ds
        return state, y

    s0 = jnp.zeros((bsz, ng, r, p, ns), f32)
    xs = tuple(jnp.moveaxis(t, 1, 0) for t in (cc, bc, xc, acum))
    _, y_off = lax.scan(step, s0, xs)
    return (y_diag + jnp.moveaxis(y_off, 0, 1)).reshape(bsz, seq, nh, p)


def gla_chunk(q, k, v, log_f, chunk):
    bsz, seq, nh, dk = q.shape
    nc = seq // chunk

    def blocks(t):
        return t.astype(jnp.float32).reshape(bsz, nc, chunk, nh, t.shape[-1]).transpose(0, 1, 3, 2, 4)

    q, k, v = blocks(q), blocks(k), blocks(v)
    gc = jnp.cumsum(blocks(log_f), axis=3)
    g_last = gc[:, :, :, -1]
    q_dec = q * jnp.exp(gc)
    k_inv = k * jnp.exp(-gc)
    k_end = k * jnp.exp(g_last[:, :, :, None] - gc)
    causal = jnp.tril(jnp.ones((chunk, chunk), dtype=bool))
    att = jnp.where(causal, jnp.einsum('bnhlk,bnhsk->bnhls', q_dec, k_inv), 0.0)
    o_intra = jnp.einsum('bnhls,bnhsv->bnhlv', att, v)

    def step(state, inp):
        qd, ke, vv, gl = inp
        o = jnp.einsum('bhlk,bhkv->bhlv', qd, state)
        state = state * jnp.exp(gl)[..., None] + jnp.einsum('bhsk,bhsv->bhkv', ke, vv)
        return state, o

    s0 = jnp.zeros((bsz, nh, dk, v.shape[-1]), jnp.float32)
    xs = tuple(jnp.moveaxis(t, 1, 0) for t in (q_dec, k_end, v, g_last))
    _, o_inter = lax.scan(step, s0, xs)
    o = o_intra + jnp.moveaxis(o_inter, 0, 1)
    return o.transpose(0, 1, 3, 2, 4).reshape(bsz, seq, nh, -1)


def gated_delta_chunk(q, k, v, g, beta, chunk):
    bsz, seq, nh, dk = q.shape
    dv = v.shape[-1]
    nc = seq // chunk

    def blocks(t):
        return t.astype(jnp.float32).reshape(bsz, nc, chunk, nh, t.shape[-1]).transpose(0, 1, 3, 2, 4)

    def blocks_s(t):
        return t.astype(jnp.float32).reshape(bsz, nc, chunk, nh).transpose(0, 1, 3, 2)

    q, k, v = blocks(q), blocks(k), blocks(v)
    beta = blocks_s(beta)
    gc = jnp.cumsum(blocks_s(g), axis=-1)
    incl = jnp.tril(jnp.ones((chunk, chunk), dtype=bool))
    strict = jnp.tril(jnp.ones((chunk, chunk), dtype=bool), k=-1)
    decay = jnp.exp(jnp.where(incl, gc[..., :, None] - gc[..., None, :], -jnp.inf))
    kb = k * beta[..., None]
    m = jnp.where(strict, jnp.einsum('bnhlk,bnhsk->bnhls', kb, k) * decay, 0.0)
    a_mat = m + jnp.eye(chunk, dtype=jnp.float32)
    rhs = jnp.concatenate([v * beta[..., None], kb * jnp.exp(gc)[..., None]], axis=-1)
    sol = lax.linalg.triangular_solve(a_mat, rhs, left_side=True, lower=True, unit_diagonal=True)
    u, w = sol[..., :dv], sol[..., dv:]
    att = jnp.einsum('bnhlk,bnhsk->bnhls', q, k) * decay
    q_dec = q * jnp.exp(gc)[..., None]
    g_last = gc[..., -1]
    k_end = k * jnp.exp(g_last[..., None] - gc)[..., None]

    def step(state, inp):
        qd, aa, uu, ww, ke, gl = inp
        v_new = uu - jnp.einsum('bhlk,bhkv->bhlv', ww, state)
        o = jnp.einsum('bhlk,bhkv->bhlv', qd, state) + jnp.einsum('bhls,bhsv->bhlv', aa, v_new)
        state = state * jnp.exp(gl)[..., None, None] + jnp.einsum('bhsk,bhsv->bhkv', ke, v_new)
        return state, o

    s0 = jnp.zeros((bsz, nh, dk, dv), jnp.float32)
    xs = tuple(jnp.moveaxis(t, 1, 0) for t in (q_dec, att, u, w, k_end, g_last))
    _, o = lax.scan(step, s0, xs)
    o = jnp.moveaxis(o, 0, 1)
    return o.transpose(0, 1, 3, 2, 4).reshape(bsz, seq, nh, dv)


def mamba2_mixer(h, in_w, conv_w, conv_b, dt_bias, a_log, d_skip, norm_w, out_w):
    bsz, seq, _ = h.shape
    f32 = jnp.float32
    proj = h @ in_w
    z = proj[..., :M_D_INNER]
    xbc = jax.nn.silu(causal_dwconv(proj[..., M_D_INNER:M_D_INNER + M_CONV_DIM], conv_w) + conv_b)
    dt = jax.nn.softplus(proj[..., M_D_INNER + M_CONV_DIM:].astype(f32) + dt_bias.astype(f32))
    xs = xbc[..., :M_D_INNER].reshape(bsz, seq, M_HEADS, M_HEAD_DIM)
    bm = xbc[..., M_D_INNER:M_D_INNER + M_GROUPS * M_STATE].reshape(bsz, seq, M_GROUPS, M_STATE)
    cm = xbc[..., M_D_INNER + M_GROUPS * M_STATE:].reshape(bsz, seq, M_GROUPS, M_STATE)
    a = -jnp.exp(a_log.astype(f32))
    y = ssd_chunk(xs, dt, a, bm, cm, SSD_CHUNK) + d_skip.astype(f32)[:, None] * xs.astype(f32)
    y = y.reshape(bsz, seq, M_D_INNER).astype(h.dtype) * jax.nn.silu(z)
    gs = M_D_INNER // M_GROUPS
    y = rmsnorm(y.reshape(bsz, seq, M_GROUPS, gs), norm_w.reshape(M_GROUPS, gs))
    return y.reshape(bsz, seq, M_D_INNER) @ out_w


def hgrn2_mixer(h, in_w, lower_bound, norm_w, out_w):
    bsz, seq, _ = h.shape
    q, f, i, g = jnp.split(h @ in_w, 4, axis=-1)

    def heads(t):
        return t.reshape(bsz, seq, H_HEADS, -1)

    lb = lower_bound.astype(jnp.float32)
    forget = lb + (1.0 - lb) * jax.nn.sigmoid(f.astype(jnp.float32))
    o = gla_chunk(heads(jax.nn.silu(q)) * H_EXPAND ** -0.5, heads(1.0 - forget), heads(i),
                  heads(jnp.log(forget)), HGRN_CHUNK)
    o = rmsnorm(o.astype(h.dtype), norm_w) * jax.nn.silu(heads(g))
    return o.reshape(bsz, seq, D_MODEL) @ out_w


def gated_deltanet_mixer(h, in_w, conv_w, a_log, dt_bias, norm_w, out_w):
    bsz, seq, _ = h.shape
    f32 = jnp.float32
    proj = h @ in_w
    qkv = jax.nn.silu(causal_dwconv(proj[..., :G_CONV_DIM], conv_w))
    z = proj[..., G_CONV_DIM:G_CONV_DIM + G_VAL_DIM]
    b = proj[..., G_CONV_DIM + G_VAL_DIM:G_CONV_DIM + G_VAL_DIM + G_V_HEADS]
    a = proj[..., G_CONV_DIM + G_VAL_DIM + G_V_HEADS:]
    q = l2norm(qkv[..., :G_KEY_DIM].reshape(bsz, seq, G_QK_HEADS, G_HEAD_DIM))
    k = l2norm(qkv[..., G_KEY_DIM:2 * G_KEY_DIM].reshape(bsz, seq, G_QK_HEADS, G_HEAD_DIM))
    v = qkv[..., 2 * G_KEY_DIM:].reshape(bsz, seq, G_V_HEADS, G_HEAD_DIM)
    rep = G_V_HEADS // G_QK_HEADS
    q = jnp.repeat(q, rep, axis=2) * G_HEAD_DIM ** -0.5
    k = jnp.repeat(k, rep, axis=2)
    beta = jax.nn.sigmoid(b.astype(f32))
    g = -jnp.exp(a_log.astype(f32)) * jax.nn.softplus(a.astype(f32) + dt_bias.astype(f32))
    o = gated_delta_chunk(q, k, v, g, beta, GDN_CHUNK)
    o = rmsnorm(o.astype(h.dtype), norm_w) * jax.nn.silu(z.reshape(bsz, seq, G_V_HEADS, G_HEAD_DIM))
    return o.reshape(bsz, seq, G_VAL_DIM) @ out_w


def memory_cross_attention(h, mem_n, wq, wkv, wo):
    bsz, seq, _ = h.shape
    q = (h @ wq).reshape(bsz, seq, X_HEADS, X_HEAD_DIM)
    k, v = jnp.split(mem_n @ wkv, 2, axis=-1)
    k = k.reshape(bsz, -1, X_HEADS, X_HEAD_DIM)
    v = v.reshape(bsz, -1, X_HEADS, X_HEAD_DIM)
    s = jnp.einsum('blhd,bmhd->bhlm', q, k).astype(jnp.float32) * X_HEAD_DIM ** -0.5
    p = jax.nn.softmax(s, axis=-1).astype(h.dtype)
    o = jnp.einsum('bhlm,bmhd->blhd', p, v).reshape(bsz, seq, D_MODEL)
    return o @ wo


def conv_glu_ffn(h, up_w, conv_w, conv_b, down_w):
    gate, up = jnp.split(h @ up_w, 2, axis=-1)
    gate = causal_dwconv(gate, conv_w) + conv_b
    return (jax.nn.silu(gate) * up) @ down_w


def setup_inputs(seed: int = 0) -> dict:
    key = jax.random.key(seed)
    keys = iter(jax.random.split(key, 48))
    d = D_MODEL
    out_scale = (2 * DEPTH) ** -0.5

    def normal(shape, scale):
        return scale * jax.random.normal(next(keys), shape, jnp.float32)

    def gain(shape):
        return 1.0 + 0.05 * jax.random.normal(next(keys), shape, jnp.float32)

    def dt_bias(shape):
        u = jax.random.uniform(next(keys), shape, jnp.float32)
        dt = jnp.exp(u * (math.log(DT_MAX) - math.log(DT_MIN)) + math.log(DT_MIN))
        return dt + jnp.log(-jnp.expm1(-dt))

    def a_log(shape):
        return jnp.log(jax.random.uniform(next(keys), shape, jnp.float32, 1.0, 16.0))

    return {
        'x': normal((BATCH, SEQ, d), 1.0),
        'mem': normal((BATCH, N_MEM, d), 1.0),
        'ln_mix': gain((DEPTH, d)),
        'ln_xattn': gain((DEPTH, d)),
        'ln_mem': gain((DEPTH, d)),
        'ln_ffn': gain((DEPTH, d)),
        'final_norm': gain((d,)),
        'm_in_w': normal((N_A, d, M_IN), d ** -0.5),
        'm_conv_w': normal((N_A, CONV_W, M_CONV_DIM), CONV_W ** -0.5),
        'm_conv_b': normal((N_A, M_CONV_DIM), 0.02),
        'm_dt_bias': dt_bias((N_A, M_HEADS)),
        'm_a_log': a_log((N_A, M_HEADS)),
        'm_d': gain((N_A, M_HEADS)),
        'm_norm_w': gain((N_A, M_D_INNER)),
        'm_out_w': normal((N_A, M_D_INNER, d), M_D_INNER ** -0.5 * out_scale),
        'h_in_w': normal((N_B, d, 4 * d), d ** -0.5),
        'h_lower_bounds': normal((DEPTH, d), 0.1),
        'h_norm_w': gain((N_B, H_DV)),
        'h_out_w': normal((N_B, d, d), d ** -0.5 * out_scale),
        'g_in_w': normal((N_C, d, G_IN), d ** -0.5),
        'g_conv_w': normal((N_C, CONV_W, G_CONV_DIM), CONV_W ** -0.5),
        'g_a_log': a_log((N_C, G_V_HEADS)),
        'g_dt_bias': dt_bias((N_C, G_V_HEADS)),
        'g_norm_w': gain((N_C, G_HEAD_DIM)),
        'g_out_w': normal((N_C, G_VAL_DIM, d), G_VAL_DIM ** -0.5 * out_scale),
        'xa_q': normal((DEPTH, d, d), d ** -0.5),
        'xa_kv': normal((DEPTH, d, 2 * d), d ** -0.5),
        'xa_o': normal((DEPTH, d, d), d ** -0.5 * out_scale),
        'f_up': normal((DEPTH, d, 2 * D_FF), d ** -0.5),
        'f_conv_w': normal((DEPTH, FFN_CONV_W, D_FF), FFN_CONV_W ** -0.5),
        'f_conv_b': normal((DEPTH, D_FF), 0.02),
        'f_down': normal((DEPTH, D_FF, d), D_FF ** -0.5 * out_scale),
    }


def reference(x, mem, ln_mix, ln_xattn, ln_mem, ln_ffn, final_norm,
              m_in_w, m_conv_w, m_conv_b, m_dt_bias, m_a_log, m_d, m_norm_w, m_out_w,
              h_in_w, h_lower_bounds, h_norm_w, h_out_w,
              g_in_w, g_conv_w, g_a_log, g_dt_bias, g_norm_w, g_out_w,
              xa_q, xa_kv, xa_o, f_up, f_conv_w, f_conv_b, f_down):
    lb = jnp.cumsum(jax.nn.softmax(h_lower_bounds.astype(jnp.float32), axis=0), axis=0)
    lb = lb - lb[:1]
    ia = 0
    ib = 0
    ic = 0
    for i in range(DEPTH):
        hn = rmsnorm(x, ln_mix[i])
        if i % N_MIXERS == 0:
            mix = mamba2_mixer(hn, m_in_w[ia], m_conv_w[ia], m_conv_b[ia], m_dt_bias[ia],
                               m_a_log[ia], m_d[ia], m_norm_w[ia], m_out_w[ia])
            ia += 1
        elif i % N_MIXERS == 1:
            mix = hgrn2_mixer(hn, h_in_w[ib], lb[i], h_norm_w[ib], h_out_w[ib])
            ib += 1
        else:
            mix = gated_deltanet_mixer(hn, g_in_w[ic], g_conv_w[ic], g_a_log[ic], g_dt_bias[ic],
                                       g_norm_w[ic], g_out_w[ic])
            ic += 1
        x = x + mix.astype(x.dtype)
        x = x + memory_cross_attention(rmsnorm(x, ln_xattn[i]), rmsnorm(mem, ln_mem[i]),
                                       xa_q[i], xa_kv[i], xa_o[i]).astype(x.dtype)
        x = x + conv_glu_ffn(rmsnorm(x, ln_ffn[i]), f_up[i], f_conv_w[i], f_conv_b[i],
                             f_down[i]).astype(x.dtype)
    return rmsnorm(x, final_norm)
```

```python
import functools

import jax
import jax.numpy as jnp
from jax import lax
from jax.experimental import pallas as pl
from jax.experimental.pallas import tpu as pltpu

F32 = jnp.float32
BF16 = jnp.bfloat16
EPS = 1e-6
NEG_INF = float("-inf")

LANES_V7X = 128
SUBLANES_V7X = 8
VMEM_LIMIT_BYTES_V7X = 56 * 1024 * 1024

CONV_W = 4
FFN_CONV_W = 3
HALO = SUBLANES_V7X

M_HEAD_DIM = 64
M_GROUPS = 8
M_STATE = 128
SSD_CHUNK = 128
H_EXPAND = 128
HGRN_CHUNK = 64
G_HEAD_DIM = 128
GDN_CHUNK = 64
GDN_QUAD = 4
X_HEADS = 4
N_MIXERS = 3


def _cparams(sem):
    return pltpu.CompilerParams(dimension_semantics=sem, vmem_limit_bytes=VMEM_LIMIT_BYTES_V7X)


def _dot(a, b):
    return jnp.dot(a, b, preferred_element_type=F32)


def _dot_nt(a, b):
    return lax.dot_general(a, b, (((1,), (1,)), ((), ())), preferred_element_type=F32)


def _dot_tn(a, b):
    return lax.dot_general(a, b, (((0,), (0,)), ((), ())), preferred_element_type=F32)


def _split3(x):
    hi = x.astype(BF16)
    r1 = x - hi.astype(F32)
    mid = r1.astype(BF16)
    lo = (r1 - mid.astype(F32)).astype(BF16)
    return hi, mid, lo


def _dot_sel_rhs(x, sel):
    hi, mid, lo = _split3(x)
    return _dot(hi, sel) + _dot(mid, sel) + _dot(lo, sel)


def _dot_sel_lhs(sel, x):
    hi, mid, lo = _split3(x)
    return _dot(sel, hi) + _dot(sel, mid) + _dot(sel, lo)


def _dot3(a, b):
    ah = a.astype(BF16)
    al = (a - ah.astype(F32)).astype(BF16)
    bh = b.astype(BF16)
    bl = (b - bh.astype(F32)).astype(BF16)
    return _dot(ah, bh) + _dot(ah, bl) + _dot(al, bh)


def _silu(x):
    return x * jax.nn.sigmoid(x)


def _softplus(x):
    return jnp.maximum(x, 0.0) + jnp.log1p(jnp.exp(-jnp.abs(x)))


def _rms_scale(x):
    return lax.rsqrt(jnp.mean(x * x, axis=-1, keepdims=True) + EPS)


def _tril_mask(n):
    row = lax.broadcasted_iota(jnp.int32, (n, n), 0)
    col = lax.broadcasted_iota(jnp.int32, (n, n), 1)
    return row >= col


def _as_sel(mask):
    return jnp.where(mask, 1.0, 0.0).astype(BF16)


def _norm_matmul_kernel(x_ref, g_ref, w_ref, o_ref, xn_ref):
    @pl.when(pl.program_id(1) == 0)
    def _():
        x = x_ref[...]
        xn_ref[...] = (x * _rms_scale(x) * g_ref[...]).astype(BF16)

    o_ref[...] = _dot(xn_ref[...], w_ref[...]).astype(o_ref.dtype)


def norm_matmul(x, gain, w, *, tm, tn, out_dtype=F32):
    t, d = x.shape
    n = w.shape[1]
    assert t % tm == 0 and n % tn == 0, (t, tm, n, tn)
    return pl.pallas_call(
        _norm_matmul_kernel,
        grid=(t // tm, n // tn),
        in_specs=[
            pl.BlockSpec((tm, d), lambda i, j: (i, 0)),
            pl.BlockSpec((1, d), lambda i, j: (0, 0)),
            pl.BlockSpec((d, tn), lambda i, j: (0, j)),
        ],
        out_specs=pl.BlockSpec((tm, tn), lambda i, j: (i, j)),
        out_shape=jax.ShapeDtypeStruct((t, n), out_dtype),
        scratch_shapes=[pltpu.VMEM((tm, d), BF16)],
        compiler_params=_cparams(("parallel", "arbitrary")),
        name="norm_matmul",
    )(x, gain.reshape(1, d), w)


def _matmul_res_kernel(a_ref, w_ref, r_ref, o_ref):
    o_ref[...] = r_ref[...] + _dot(a_ref[...], w_ref[...])


def _matmul_res_norm_kernel(a_ref, w_ref, r_ref, g_ref, o_ref):
    y = r_ref[...] + _dot(a_ref[...], w_ref[...])
    o_ref[...] = y * _rms_scale(y) * g_ref[...]


def matmul_residual(a, w, res, *, tm, final_gain=None):
    t, k = a.shape
    d = w.shape[1]
    assert t % tm == 0
    in_specs = [
        pl.BlockSpec((tm, k), lambda i: (i, 0)),
        pl.BlockSpec((k, d), lambda i: (0, 0)),
        pl.BlockSpec((tm, d), lambda i: (i, 0)),
    ]
    args = [a, w, res]
    body = _matmul_res_kernel
    if final_gain is not None:
        in_specs.append(pl.BlockSpec((1, d), lambda i: (0, 0)))
        args.append(final_gain.reshape(1, d))
        body = _matmul_res_norm_kernel
    return pl.pallas_call(
        body,
        grid=(t // tm,),
        in_specs=in_specs,
        out_specs=pl.BlockSpec((tm, d), lambda i: (i, 0)),
        out_shape=jax.ShapeDtypeStruct((t, d), F32),
        compiler_params=_cparams(("parallel",)),
        name="matmul_residual",
    )(*args)


def _xattn_kernel(x_ref, g_ref, wq_ref, kv_ref, wo_ref, o_ref, *, heads):
    x = x_ref[...]
    d = x.shape[1]
    dh = d // heads
    xn = (x * _rms_scale(x) * g_ref[...]).astype(BF16)
    q = (_dot(xn, wq_ref[...]) * (dh ** -0.5)).astype(BF16)
    outs = []
    for h in range(heads):
        k_h = kv_ref[:, h * dh:(h + 1) * dh]
        v_h = kv_ref[:, d + h * dh:d + (h + 1) * dh]
        s = _dot_nt(q[:, h * dh:(h + 1) * dh], k_h)
        p = jnp.exp(s - jnp.max(s, axis=-1, keepdims=True))
        o_h = _dot(p.astype(BF16), v_h) / jnp.sum(p, axis=-1, keepdims=True)
        outs.append(o_h.astype(BF16))
    o_ref[...] = x + _dot(jnp.concatenate(outs, axis=1), wo_ref[...])


def memory_cross_attention(x, gain, wq, kv, wo, *, bsz, seq, n_mem, tq):
    t, d = x.shape
    nq = seq // tq
    return pl.pallas_call(
        functools.partial(_xattn_kernel, heads=X_HEADS),
        grid=(bsz, nq),
        in_specs=[
            pl.BlockSpec((tq, d), lambda b, i: (b * nq + i, 0)),
            pl.BlockSpec((1, d), lambda b, i: (0, 0)),
            pl.BlockSpec((d, d), lambda b, i: (0, 0)),
            pl.BlockSpec((n_mem, 2 * d), lambda b, i: (b, 0)),
            pl.BlockSpec((d, d), lambda b, i: (0, 0)),
        ],
        out_specs=pl.BlockSpec((tq, d), lambda b, i: (b * nq + i, 0)),
        out_shape=jax.ShapeDtypeStruct((t, d), F32),
        compiler_params=_cparams(("parallel", "parallel")),
        name="memory_xattn",
    )(x, gain.reshape(1, d), wq, kv, wo)


def _ffn_up_kernel(x_ref, g_ref, wg_ref, wu_ref, cw_ref, cb_ref, o_ref, xn_ref, gp_ref, halo_ref):
    i = pl.program_id(1)
    j = pl.program_id(2)
    tm = x_ref.shape[0]

    @pl.when(j == 0)
    def _():
        x = x_ref[...]
        xn_ref[...] = (x * _rms_scale(x) * g_ref[...]).astype(BF16)

    xn = xn_ref[...]
    gate = _dot(xn, wg_ref[...])
    up = _dot(xn, wu_ref[...])

    @pl.when(i == 0)
    def _():
        gp_ref[0:HALO, :] = jnp.zeros((HALO, gate.shape[1]), F32)

    @pl.when(i > 0)
    def _():
        gp_ref[0:HALO, :] = halo_ref[j]

    gp_ref[HALO:HALO + tm, :] = gate
    halo_ref[j] = gate[tm - HALO:tm, :]
    acc = cb_ref[...] + cw_ref[FFN_CONV_W - 1:FFN_CONV_W, :] * gate
    for tap in range(FFN_CONV_W - 1):
        off = HALO - (FFN_CONV_W - 1) + tap
        acc = acc + cw_ref[tap:tap + 1, :] * gp_ref[off:off + tm, :]
    o_ref[...] = (_silu(acc) * up).astype(o_ref.dtype)


def ffn_up(x, gain, wg, wu, conv_w, conv_b, *, bsz, seq, tm, tn):
    t, d = x.shape
    f = wg.shape[1]
    assert seq % tm == 0 and f % tn == 0
    ni, nj = seq // tm, f // tn
    return pl.pallas_call(
        _ffn_up_kernel,
        grid=(bsz, ni, nj),
        in_specs=[
            pl.BlockSpec((tm, d), lambda b, i, j: (b * ni + i, 0)),
            pl.BlockSpec((1, d), lambda b, i, j: (0, 0)),
            pl.BlockSpec((d, tn), lambda b, i, j: (0, j)),
            pl.BlockSpec((d, tn), lambda b, i, j: (0, j)),
            pl.BlockSpec((FFN_CONV_W, tn), lambda b, i, j: (0, j)),
            pl.BlockSpec((1, tn), lambda b, i, j: (0, j)),
        ],
        out_specs=pl.BlockSpec((tm, tn), lambda b, i, j: (b * ni + i, j)),
        out_shape=jax.ShapeDtypeStruct((t, f), BF16),
        scratch_shapes=[
            pltpu.VMEM((tm, d), BF16),
            pltpu.VMEM((tm + HALO, tn), F32),
            pltpu.VMEM((nj, HALO, tn), F32),
        ],
        compiler_params=_cparams(("parallel", "arbitrary", "arbitrary")),
        name="ffn_up",
    )(x, gain.reshape(1, d), wg, wu, conv_w, conv_b.reshape(1, f))


def _load_conv_window(xp_ref, x_ref, c, q):
    @pl.when(c == 0)
    def _():
        xp_ref[0:HALO, :] = jnp.zeros((HALO, xp_ref.shape[1]), F32)

    @pl.when(c > 0)
    def _():
        xp_ref[0:HALO, :] = xp_ref[q:q + HALO, :]

    xp_ref[HALO:HALO + q, :] = x_ref[...]


def _causal_conv(xp_ref, cw_ref, lo, width, q):
    acc = None
    for tap in range(CONV_W):
        off = HALO - (CONV_W - 1) + tap
        term = cw_ref[tap:tap + 1, lo:lo + width] * xp_ref[off:off + q, lo:lo + width]
        acc = term if acc is None else acc + term
    return acc


def _ssd_kernel(xbc_ref, z_ref, dt_ref, cw_ref, cb_ref, dtb_ref, alog_ref, dskip_ref, nw_ref,
                ehead_ref, ecol_ref, o_ref, xp_ref, s_ref, *, q, groups, inner, state):
    c = pl.program_id(1)
    gw = inner // groups
    hpg = gw // M_HEAD_DIM
    _load_conv_window(xp_ref, xbc_ref, c, q)

    @pl.when(c == 0)
    def _():
        s_ref[...] = jnp.zeros_like(s_ref)

    dt = _softplus(dt_ref[...] + dtb_ref[...])
    a_neg = -jnp.exp(alog_ref[...])
    tril = _tril_mask(q)
    acum = _dot_sel_lhs(_as_sel(tril), dt * a_neg)
    acum_t = acum.T
    lane_head = lax.broadcasted_iota(jnp.int32, (q, gw), 1) // M_HEAD_DIM

    for g in range(groups):
        xs = _silu(_causal_conv(xp_ref, cw_ref, g * gw, gw, q) + cb_ref[:, g * gw:(g + 1) * gw])
        b_lo = inner + g * state
        c_lo = inner + groups * state + g * state
        bm = _silu(_causal_conv(xp_ref, cw_ref, b_lo, state, q) + cb_ref[:, b_lo:b_lo + state])
        cm = _silu(_causal_conv(xp_ref, cw_ref, c_lo, state, q) + cb_ref[:, c_lo:c_lo + state])
        bm16 = bm.astype(BF16)
        cm16 = cm.astype(BF16)
        e_g = ehead_ref[:, g * gw:(g + 1) * gw]
        dt_x = _dot_sel_rhs(dt, e_g)
        ac_x = _dot_sel_rhs(acum, e_g)
        ac_col = _dot_sel_rhs(acum, ecol_ref[:, g * hpg * q:(g + 1) * hpg * q])
        xdt = xs * dt_x
        cb = _dot_nt(cm16, bm16)
        lhs, rhs = [], []
        for hh in range(hpg):
            h = g * hpg + hh
            diff = ac_col[:, hh * q:(hh + 1) * q] - acum_t[h:h + 1, :]
            dec = jnp.exp(jnp.where(tril, diff, NEG_INF))
            lhs.append((cb * dec).astype(BF16))
            rhs.append(jnp.where(lane_head == hh, xdt, 0.0).astype(BF16))
        y = _dot(jnp.concatenate(lhs, axis=1), jnp.concatenate(rhs, axis=0))
        s_old = s_ref[g]
        y = y + _dot(cm16, s_old.astype(BF16)) * jnp.exp(ac_x)
        y = y + dskip_ref[:, g * gw:(g + 1) * gw] * xs
        last = ac_x[q - 1:q, :]
        xw = (xdt * jnp.exp(last - ac_x)).astype(BF16)
        s_ref[g] = s_old * jnp.exp(last) + _dot_tn(bm16, xw)
        y = y * _silu(z_ref[:, g * gw:(g + 1) * gw])
        o_ref[:, g * gw:(g + 1) * gw] = (y * _rms_scale(y) * nw_ref[:, g * gw:(g + 1) * gw]).astype(BF16)


def ssd_mixer_core(proj, conv_w, conv_b, dt_bias, a_log, d_skip, norm_w, *, bsz, seq, inner, heads):
    t = proj.shape[0]
    q = SSD_CHUNK
    nc = seq // q
    conv_dim = inner + 2 * M_GROUPS * M_STATE
    assert conv_dim == 2 * inner and heads <= LANES_V7X
    pad = LANES_V7X - heads
    head_ids = jnp.arange(LANES_V7X)
    ehead = (head_ids[:, None] == (jnp.arange(inner) // M_HEAD_DIM)[None, :]).astype(BF16)
    ecol = (head_ids[:, None] == (jnp.arange(heads * q) // q)[None, :]).astype(BF16)
    z_blk = conv_dim // inner
    dt_blk = (conv_dim + inner) // LANES_V7X
    row = lambda b, c: b * nc + c
    const = lambda shape: pl.BlockSpec(shape, lambda b, c: (0, 0))
    return pl.pallas_call(
        functools.partial(_ssd_kernel, q=q, groups=M_GROUPS, inner=inner, state=M_STATE),
        grid=(bsz, nc),
        in_specs=[
            pl.BlockSpec((q, conv_dim), lambda b, c: (row(b, c), 0)),
            pl.BlockSpec((q, inner), lambda b, c: (row(b, c), z_blk)),
            pl.BlockSpec((q, LANES_V7X), lambda b, c: (row(b, c), dt_blk)),
            const((CONV_W, conv_dim)),
            const((1, conv_dim)),
            const((1, LANES_V7X)),
            const((1, LANES_V7X)),
            const((1, inner)),
            const((1, inner)),
            const((LANES_V7X, inner)),
            const((LANES_V7X, heads * q)),
        ],
        out_specs=pl.BlockSpec((q, inner), lambda b, c: (row(b, c), 0)),
        out_shape=jax.ShapeDtypeStruct((t, inner), BF16),
        scratch_shapes=[
            pltpu.VMEM((q + HALO, conv_dim), F32),
            pltpu.VMEM((M_GROUPS, M_STATE, inner // M_GROUPS), F32),
        ],
        compiler_params=_cparams(("parallel", "arbitrary")),
        name="ssd_core",
    )(proj, proj, proj, conv_w, conv_b.reshape(1, conv_dim),
      jnp.pad(dt_bias, (0, pad)).reshape(1, LANES_V7X), jnp.pad(a_log, (0, pad)).reshape(1, LANES_V7X),
      jnp.repeat(d_skip, M_HEAD_DIM).reshape(1, inner), norm_w.reshape(1, inner), ehead, ecol)


def _hgrn2_kernel(q_ref, f_ref, i_ref, g_ref, lb_ref, nw_ref, o_ref, s_ref, *, q, heads):
    c = pl.program_id(1)
    dk = H_EXPAND

    @pl.when(c == 0)
    def _():
        s_ref[...] = jnp.zeros_like(s_ref)

    lb = lb_ref[...]
    forget = lb + (1.0 - lb) * jax.nn.sigmoid(f_ref[...])
    tril = _tril_mask(q)
    gc = _dot_sel_lhs(_as_sel(tril), jnp.log(forget))
    key = 1.0 - forget
    qs = _silu(q_ref[...]) * (dk ** -0.5)
    mid = q // 2 - 1
    g_mid = gc[mid:mid + 1, :]
    g_last = gc[q - 1:q, :]
    q_dec = (qs * jnp.exp(gc - g_mid)).astype(BF16)
    k_inv = (key * jnp.exp(g_mid - gc)).astype(BF16)
    q_in = (qs * jnp.exp(gc)).astype(BF16)
    k_end = (key * jnp.exp(g_last - gc)).astype(BF16)
    e_last = jnp.exp(g_last)
    for h in range(heads):
        sl = slice(h * dk, (h + 1) * dk)
        v_h = i_ref[:, sl].astype(BF16)
        att = jnp.where(tril, _dot_nt(q_dec[:, sl], k_inv[:, sl]), 0.0)
        st_old = s_ref[h]
        o_h = _dot(att.astype(BF16), v_h) + _dot_nt(q_in[:, sl], st_old.astype(BF16))
        s_ref[h] = st_old * e_last[:, sl] + _dot_tn(v_h, k_end[:, sl])
        o_h = o_h * _rms_scale(o_h) * nw_ref[...]
        o_ref[:, sl] = (o_h * _silu(g_ref[:, sl])).astype(BF16)


def hgrn2_mixer_core(proj, lower_bound, norm_w, *, bsz, seq, d):
    t = proj.shape[0]
    q = HGRN_CHUNK
    nc = seq // q
    heads = d // H_EXPAND
    row = lambda b, c: b * nc + c
    part = lambda k: pl.BlockSpec((q, d), lambda b, c: (row(b, c), k))
    return pl.pallas_call(
        functools.partial(_hgrn2_kernel, q=q, heads=heads),
        grid=(bsz, nc),
        in_specs=[part(0), part(1), part(2), part(3),
                  pl.BlockSpec((1, d), lambda b, c: (0, 0)),
                  pl.BlockSpec((1, H_EXPAND), lambda b, c: (0, 0))],
        out_specs=pl.BlockSpec((q, d), lambda b, c: (row(b, c), 0)),
        out_shape=jax.ShapeDtypeStruct((t, d), BF16),
        scratch_shapes=[pltpu.VMEM((heads, H_EXPAND, d // heads), F32)],
        compiler_params=_cparams(("parallel", "arbitrary")),
        name="hgrn2_core",
    )(proj, proj, proj, proj, lower_bound.reshape(1, d), norm_w.reshape(1, H_EXPAND))


def _block_diag(blocks):
    n = len(blocks)
    rows = []
    for e, blk in enumerate(blocks):
        zero = jnp.zeros_like(blk)
        rows.append(jnp.concatenate([blk if k == e else zero for k in range(n)], axis=1))
    return jnp.concatenate(rows, axis=0)


def _gdn_kernel(qkv_ref, z_ref, b_ref, a_ref, cw_ref, dtb_ref, alog_ref, nw_ref, e64_ref, e128_ref,
                o_ref, xp_ref, s_ref, *, q, qk_heads, v_heads):
    c = pl.program_id(1)
    dh = G_HEAD_DIM
    key_dim = qk_heads * dh
    rep = v_heads // qk_heads
    _load_conv_window(xp_ref, qkv_ref, c, q)

    @pl.when(c == 0)
    def _():
        s_ref[...] = jnp.zeros_like(s_ref)

    tril = _tril_mask(q)
    beta = jax.nn.sigmoid(b_ref[...])
    gate = -jnp.exp(alog_ref[...]) * _softplus(a_ref[...] + dtb_ref[...])
    gc = _dot_sel_lhs(_as_sel(tril), gate)

    eye_t = (lax.broadcasted_iota(jnp.int32, (q, v_heads * q), 0)
             == lax.broadcasted_iota(jnp.int32, (q, v_heads * q), 1) % q)
    col_s = lax.broadcasted_iota(jnp.int32, (q, v_heads * q), 1) % q
    row_l = lax.broadcasted_iota(jnp.int32, (q, v_heads * q), 0)
    g_col = _dot_sel_rhs(gc, e64_ref[...])
    g_row = jnp.sum(jnp.where(eye_t, g_col, 0.0), axis=0, keepdims=True)
    decay = jnp.exp(jnp.where(row_l >= col_s, g_col - g_row, NEG_INF))
    b_col = _dot_sel_rhs(beta, e64_ref[...])

    kk_parts, qk_parts, k_heads, q_heads = [], [], [], []
    for p in range(qk_heads):
        q_p = _silu(_causal_conv(xp_ref, cw_ref, p * dh, dh, q))
        k_p = _silu(_causal_conv(xp_ref, cw_ref, key_dim + p * dh, dh, q))
        q_p = q_p * lax.rsqrt(jnp.sum(q_p * q_p, axis=-1, keepdims=True) + EPS) * (dh ** -0.5)
        k_p = k_p * lax.rsqrt(jnp.sum(k_p * k_p, axis=-1, keepdims=True) + EPS)
        k16 = k_p.astype(BF16)
        k_rep = jnp.concatenate([k16] * rep, axis=0)
        kk_parts.append(_dot_nt(k16, k_rep))
        qk_parts.append(_dot_nt(q_p.astype(BF16), k_rep))
        k_heads.append(k_p)
        q_heads.append(q_p)
    kk = jnp.concatenate(kk_parts, axis=1)
    qk = jnp.concatenate(qk_parts, axis=1)
    m_all = jnp.where(row_l > col_s, b_col * kk * decay, 0.0)
    att_all = qk * decay

    quad_w = GDN_QUAD * q
    lane_blk = lax.broadcasted_iota(jnp.int32, (q, quad_w), 1) // q
    for u in range(v_heads // GDN_QUAD):
        p_pow = -m_all[:, u * quad_w:(u + 1) * quad_w]
        n_acc = p_pow
        for _ in range(q.bit_length() - 2):
            bd = jnp.concatenate([jnp.where(lane_blk == e, p_pow, 0.0) for e in range(GDN_QUAD)], axis=0)
            p_pow = _dot3(p_pow, bd)
            bd2 = jnp.concatenate([jnp.where(lane_blk == e, p_pow, 0.0) for e in range(GDN_QUAD)], axis=0)
            n_acc = n_acc + p_pow + _dot3(n_acc, bd2)
        att_u = att_all[:, u * quad_w:(u + 1) * quad_w]

        vb, kbg, qd, ke, el = [], [], [], [], []
        for e in range(GDN_QUAD):
            h = u * GDN_QUAD + e
            p = h // rep
            e_h = e128_ref[:, h * dh:(h + 1) * dh]
            g_x = _dot_sel_rhs(gc, e_h)
            b_x = _dot_sel_rhs(beta, e_h)
            eg = jnp.exp(g_x)
            g_l = g_x[q - 1:q, :]
            v_h = _silu(_causal_conv(xp_ref, cw_ref, 2 * key_dim + h * dh, dh, q))
            vb.append(v_h * b_x)
            kbg.append(k_heads[p] * b_x * eg)
            qd.append(q_heads[p] * eg)
            ke.append(k_heads[p] * jnp.exp(g_l - g_x))
            el.append(jnp.exp(g_l))
        rhs = _block_diag([jnp.concatenate([vb[e], kbg[e]], axis=1).astype(BF16) for e in range(GDN_QUAD)])
        sol = _dot(n_acc.astype(BF16), rhs)
        v_new = []
        o_inter = []
        for e in range(GDN_QUAD):
            h = u * GDN_QUAD + e
            u_h = vb[e] + sol[:, e * 2 * dh:e * 2 * dh + dh]
            w_h = kbg[e] + sol[:, e * 2 * dh + dh:(e + 1) * 2 * dh]
            s16 = s_ref[h].astype(BF16)
            both = _dot(jnp.concatenate([w_h, qd[e]], axis=0).astype(BF16), s16)
            v_new.append(u_h - both[0:q, :])
            o_inter.append(both[q:2 * q, :])
        o_intra = _dot(att_u.astype(BF16), _block_diag([v.astype(BF16) for v in v_new]))
        for e in range(GDN_QUAD):
            h = u * GDN_QUAD + e
            sl = slice(h * dh, (h + 1) * dh)
            s_ref[h] = s_ref[h] * el[e] + _dot_tn(ke[e].astype(BF16), v_new[e].astype(BF16))
            o_h = o_inter[e] + o_intra[:, e * dh:(e + 1) * dh]
            o_h = o_h * _rms_scale(o_h) * nw_ref[...]
            o_ref[:, sl] = (o_h * _silu(z_ref[:, sl])).astype(BF16)


def gdn_mixer_core(proj, conv_w, a_log, dt_bias, norm_w, *, bsz, seq, qk_heads, v_heads):
    t = proj.shape[0]
    q = GDN_CHUNK
    nc = seq // q
    dh = G_HEAD_DIM
    key_dim = qk_heads * dh
    val_dim = v_heads * dh
    conv_dim = 2 * key_dim + val_dim
    assert conv_dim == 2 * val_dim and v_heads % GDN_QUAD == 0
    pad = LANES_V7X - v_heads
    head_ids = jnp.arange(LANES_V7X)
    e64 = (head_ids[:, None] == (jnp.arange(v_heads * q) // q)[None, :]).astype(BF16)
    e128 = (head_ids[:, None] == (jnp.arange(val_dim) // dh)[None, :]).astype(BF16)
    z_blk = conv_dim // val_dim
    b_blk = (conv_dim + val_dim) // LANES_V7X
    row = lambda b, c: b * nc + c
    const = lambda shape: pl.BlockSpec(shape, lambda b, c: (0, 0))
    return pl.pallas_call(
        functools.partial(_gdn_kernel, q=q, qk_heads=qk_heads, v_heads=v_heads),
        grid=(bsz, nc),
        in_specs=[
            pl.BlockSpec((q, conv_dim), lambda b, c: (row(b, c), 0)),
            pl.BlockSpec((q, val_dim), lambda b, c: (row(b, c), z_blk)),
            pl.BlockSpec((q, LANES_V7X), lambda b, c: (row(b, c), b_blk)),
            pl.BlockSpec((q, LANES_V7X), lambda b, c: (row(b, c), b_blk + 1)),
            const((CONV_W, conv_dim)),
            const((1, LANES_V7X)),
            const((1, LANES_V7X)),
            const((1, dh)),
            const((LANES_V7X, v_heads * q)),
            const((LANES_V7X, val_dim)),
        ],
        out_specs=pl.BlockSpec((q, val_dim), lambda b, c: (row(b, c), 0)),
        out_shape=jax.ShapeDtypeStruct((t, val_dim), BF16),
        scratch_shapes=[
            pltpu.VMEM((q + HALO, conv_dim), F32),
            pltpu.VMEM((v_heads, dh, dh), F32),
        ],
        compiler_params=_cparams(("parallel", "arbitrary")),
        name="gdn_core",
    )(proj, proj, proj, proj, conv_w, jnp.pad(dt_bias, (0, pad)).reshape(1, LANES_V7X),
      jnp.pad(a_log, (0, pad)).reshape(1, LANES_V7X), norm_w.reshape(1, dh), e64, e128)


def _pad_cols(w, width):
    return jnp.pad(w, ((0, 0), (0, width - w.shape[1])))


def _ssd_in_weight(in_w, inner, conv_dim):
    z = in_w[:, :inner]
    xbc = in_w[:, inner:inner + conv_dim]
    dt = _pad_cols(in_w[:, inner + conv_dim:], LANES_V7X)
    return jnp.concatenate([xbc, z, dt], axis=1).astype(BF16)


def _gdn_in_weight(in_w, conv_dim, val_dim, v_heads):
    qkv = in_w[:, :conv_dim]
    z = in_w[:, conv_dim:conv_dim + val_dim]
    b = _pad_cols(in_w[:, conv_dim + val_dim:conv_dim + val_dim + v_heads], LANES_V7X)
    a = _pad_cols(in_w[:, conv_dim + val_dim + v_heads:], LANES_V7X)
    return jnp.concatenate([qkv, z, b, a], axis=1).astype(BF16)


def _pick_tile(n, candidates):
    for cand in candidates:
        if n % cand == 0:
            return cand
    raise ValueError(f"no tile for {n} in {candidates}")


def kernel(x, mem, ln_mix, ln_xattn, ln_mem, ln_ffn, final_norm, m_in_w, m_conv_w, m_conv_b, m_dt_bias, m_a_log, m_d, m_norm_w, m_out_w, h_in_w, h_lower_bounds, h_norm_w, h_out_w, g_in_w, g_conv_w, g_a_log, g_dt_bias, g_norm_w, g_out_w, xa_q, xa_kv, xa_o, f_up, f_conv_w, f_conv_b, f_down):
    bsz, seq, d = x.shape
    n_mem = mem.shape[1]
    depth = ln_mix.shape[0]
    t = bsz * seq
    tm = _pick_tile(t, (1024, 512, 256, 128, 64))
    tq = _pick_tile(seq, (512, 256, 128, 64))

    lb = jnp.cumsum(jax.nn.softmax(h_lower_bounds.astype(F32), axis=0), axis=0)
    lb = lb - lb[:1]

    m_inner = m_out_w.shape[1]
    m_heads = m_dt_bias.shape[1]
    m_conv_dim = m_conv_w.shape[2]
    g_v_heads = g_a_log.shape[1]
    g_val_dim = g_out_w.shape[1]
    g_conv_dim = g_conv_w.shape[2]
    g_qk_heads = (g_conv_dim - g_val_dim) // (2 * G_HEAD_DIM)
    d_ff = f_down.shape[1]

    xf = x.reshape(t, d)
    memf = mem.reshape(bsz * n_mem, d)
    ia = ib = ic = 0
    out = None
    for i in range(depth):
        kind = i % N_MIXERS
        if kind == 0:
            w_in = _ssd_in_weight(m_in_w[ia], m_inner, m_conv_dim)
            proj = norm_matmul(xf, ln_mix[i], w_in, tm=tm, tn=_pick_tile(w_in.shape[1], (896, 128)))
            y = ssd_mixer_core(proj, m_conv_w[ia], m_conv_b[ia], m_dt_bias[ia], m_a_log[ia], m_d[ia],
                               m_norm_w[ia], bsz=bsz, seq=seq, inner=m_inner, heads=m_heads)
            xf = matmul_residual(y, m_out_w[ia].astype(BF16), xf, tm=tq)
            ia += 1
        elif kind == 1:
            proj = norm_matmul(xf, ln_mix[i], h_in_w[ib].astype(BF16), tm=tm, tn=1024)
            y = hgrn2_mixer_core(proj, lb[i], h_norm_w[ib], bsz=bsz, seq=seq, d=d)
            xf = matmul_residual(y, h_out_w[ib].astype(BF16), xf, tm=tq)
            ib += 1
        else:
            w_in = _gdn_in_weight(g_in_w[ic], g_conv_dim, g_val_dim, g_v_heads)
            proj = norm_matmul(xf, ln_mix[i], w_in, tm=tm, tn=_pick_tile(w_in.shape[1], (1280, 128)))
            y = gdn_mixer_core(proj, g_conv_w[ic], g_a_log[ic], g_dt_bias[ic], g_norm_w[ic],
                               bsz=bsz, seq=seq, qk_heads=g_qk_heads, v_heads=g_v_heads)
            xf = matmul_residual(y, g_out_w[ic].astype(BF16), xf, tm=tq)
            ic += 1

        kv = norm_matmul(memf, ln_mem[i], xa_kv[i].astype(BF16), tm=_pick_tile(bsz * n_mem, (1024, 512, 256)),
                         tn=1024, out_dtype=BF16)
        xf = memory_cross_attention(xf, ln_xattn[i], xa_q[i].astype(BF16), kv, xa_o[i].astype(BF16),
                                    bsz=bsz, seq=seq, n_mem=n_mem, tq=tq)

        up_w = f_up[i].astype(BF16)
        act = ffn_up(xf, ln_ffn[i], up_w[:, :d_ff], up_w[:, d_ff:], f_conv_w[i], f_conv_b[i],
                     bsz=bsz, seq=seq, tm=tq, tn=_pick_tile(d_ff, (1408, 256, 128)))
        last = i == depth - 1
        xf = matmul_residual(act, f_down[i].astype(BF16), xf, tm=tq,
                             final_gain=final_norm if last else None)
    return xf.reshape(bsz, seq, d)
```

```python
import functools

import jax
import jax.numpy as jnp
from jax import lax
from jax.experimental import pallas as pl
from jax.experimental.pallas import tpu as pltpu

F32 = jnp.float32
BF16 = jnp.bfloat16
EPS = 1e-6
NEG_INF = float("-inf")

LANES_V7X = 128
SUBLANES_V7X = 8
VMEM_LIMIT_BYTES_V7X = 56 * 1024 * 1024

CONV_W = 4
FFN_CONV_W = 3
HALO = SUBLANES_V7X

M_HEAD_DIM = 64
M_GROUPS = 8
M_STATE = 128
SSD_CHUNK = 128
H_EXPAND = 128
HGRN_CHUNK = 64
G_HEAD_DIM = 128
GDN_CHUNK = 64
GDN_QUAD = 4
X_HEADS = 4
N_MIXERS = 3


def _cparams(sem):
    return pltpu.CompilerParams(dimension_semantics=sem, vmem_limit_bytes=VMEM_LIMIT_BYTES_V7X)


def _dot(a, b):
    return jnp.dot(a, b, preferred_element_type=F32)


def _dot_nt(a, b):
    return lax.dot_general(a, b, (((1,), (1,)), ((), ())), preferred_element_type=F32)


def _dot_tn(a, b):
    return lax.dot_general(a, b, (((0,), (0,)), ((), ())), preferred_element_type=F32)


def _split3(x):
    hi = x.astype(BF16)
    r1 = x - hi.astype(F32)
    mid = r1.astype(BF16)
    lo = (r1 - mid.astype(F32)).astype(BF16)
    return hi, mid, lo


def _dot_sel_lhs(sel, x):
    hi, mid, lo = _split3(x)
    return _dot(sel, hi) + _dot(sel, mid) + _dot(sel, lo)


def _silu(x):
    return x * jax.nn.sigmoid(x)


def _softplus(x):
    return jnp.maximum(x, 0.0) + jnp.log1p(jnp.exp(-jnp.abs(x)))


def _rms_scale(x):
    return lax.rsqrt(jnp.mean(x * x, axis=-1, keepdims=True) + EPS)


def _tril_mask(n):
    row = lax.broadcasted_iota(jnp.int32, (n, n), 0)
    col = lax.broadcasted_iota(jnp.int32, (n, n), 1)
    return row >= col


def _as_sel(mask):
    return jnp.where(mask, 1.0, 0.0).astype(BF16)


def _norm_matmul_kernel(x_ref, g_ref, w_ref, o_ref, xn_ref):
    @pl.when(pl.program_id(1) == 0)
    def _():
        x = x_ref[...]
        xn_ref[...] = (x * _rms_scale(x) * g_ref[...]).astype(BF16)

    o_ref[...] = _dot(xn_ref[...], w_ref[...]).astype(o_ref.dtype)


def norm_matmul(x, gain, w, *, tm, tn, out_dtype=F32):
    t, d = x.shape
    n = w.shape[1]
    assert t % tm == 0 and n % tn == 0, (t, tm, n, tn)
    return pl.pallas_call(
        _norm_matmul_kernel,
        grid=(t // tm, n // tn),
        in_specs=[
            pl.BlockSpec((tm, d), lambda i, j: (i, 0)),
            pl.BlockSpec((1, d), lambda i, j: (0, 0)),
            pl.BlockSpec((d, tn), lambda i, j: (0, j)),
        ],
        out_specs=pl.BlockSpec((tm, tn), lambda i, j: (i, j)),
        out_shape=jax.ShapeDtypeStruct((t, n), out_dtype),
        scratch_shapes=[pltpu.VMEM((tm, d), BF16)],
        compiler_params=_cparams(("parallel", "arbitrary")),
        name="norm_matmul",
    )(x, gain.reshape(1, d), w)


def _matmul_res_kernel(a_ref, w_ref, r_ref, o_ref):
    o_ref[...] = r_ref[...] + _dot(a_ref[...], w_ref[...])


def _matmul_res_norm_kernel(a_ref, w_ref, r_ref, g_ref, o_ref):
    y = r_ref[...] + _dot(a_ref[...], w_ref[...])
    o_ref[...] = y * _rms_scale(y) * g_ref[...]


def matmul_residual(a, w, res, *, tm, final_gain=None):
    t, k = a.shape
    d = w.shape[1]
    assert t % tm == 0
    in_specs = [
        pl.BlockSpec((tm, k), lambda i: (i, 0)),
        pl.BlockSpec((k, d), lambda i: (0, 0)),
        pl.BlockSpec((tm, d), lambda i: (i, 0)),
    ]
    args = [a, w, res]
    body = _matmul_res_kernel
    if final_gain is not None:
        in_specs.append(pl.BlockSpec((1, d), lambda i: (0, 0)))
        args.append(final_gain.reshape(1, d))
        body = _matmul_res_norm_kernel
    return pl.pallas_call(
        body,
        grid=(t // tm,),
        in_specs=in_specs,
        out_specs=pl.BlockSpec((tm, d), lambda i: (i, 0)),
        out_shape=jax.ShapeDtypeStruct((t, d), F32),
        compiler_params=_cparams(("parallel",)),
        name="matmul_residual",
    )(*args)


def _xattn_kernel(x_ref, g_ref, wq_ref, kv_ref, wo_ref, o_ref, *, heads):
    x = x_ref[...]
    d = x.shape[1]
    dh = d // heads
    xn = (x * _rms_scale(x) * g_ref[...]).astype(BF16)
    q = (_dot(xn, wq_ref[...]) * (dh ** -0.5)).astype(BF16)
    outs = []
    for h in range(heads):
        k_h = kv_ref[:, h * dh:(h + 1) * dh]
        v_h = kv_ref[:, d + h * dh:d + (h + 1) * dh]
        s = _dot_nt(q[:, h * dh:(h + 1) * dh], k_h)
        p = jnp.exp(s - jnp.max(s, axis=-1, keepdims=True))
        o_h = _dot(p.astype(BF16), v_h) / jnp.sum(p, axis=-1, keepdims=True)
        outs.append(o_h.astype(BF16))
    o_ref[...] = x + _dot(jnp.concatenate(outs, axis=1), wo_ref[...])


def memory_cross_attention(x, gain, wq, kv, wo, *, bsz, seq, n_mem, tq):
    t, d = x.shape
    nq = seq // tq
    return pl.pallas_call(
        functools.partial(_xattn_kernel, heads=X_HEADS),
        grid=(bsz, nq),
        in_specs=[
            pl.BlockSpec((tq, d), lambda b, i: (b * nq + i, 0)),
            pl.BlockSpec((1, d), lambda b, i: (0, 0)),
            pl.BlockSpec((d, d), lambda b, i: (0, 0)),
            pl.BlockSpec((n_mem, 2 * d), lambda b, i: (b, 0)),
            pl.BlockSpec((d, d), lambda b, i: (0, 0)),
        ],
        out_specs=pl.BlockSpec((tq, d), lambda b, i: (b * nq + i, 0)),
        out_shape=jax.ShapeDtypeStruct((t, d), F32),
        compiler_params=_cparams(("parallel", "parallel")),
        name="memory_xattn",
    )(x, gain.reshape(1, d), wq, kv, wo)


def _ffn_up_kernel(x_ref, g_ref, wg_ref, wu_ref, cw_ref, cb_ref, o_ref, xn_ref, gp_ref, halo_ref):
    i = pl.program_id(1)
    j = pl.program_id(2)
    tm = x_ref.shape[0]

    @pl.when(j == 0)
    def _():
        x = x_ref[...]
        xn_ref[...] = (x * _rms_scale(x) * g_ref[...]).astype(BF16)

    @pl.when(i == 0)
    def _():
        gp_ref[0:HALO, :] = jnp.zeros((HALO, gp_ref.shape[1]), F32)

    @pl.when(i > 0)
    def _():
        gp_ref[0:HALO, :] = halo_ref[j]

    xn = xn_ref[...]
    gate = _dot(xn, wg_ref[...])
    up = _dot(xn, wu_ref[...])
    gp_ref[HALO:HALO + tm, :] = gate
    halo_ref[j] = gate[tm - HALO:tm, :]
    acc = cb_ref[...] + cw_ref[FFN_CONV_W - 1:FFN_CONV_W, :] * gate
    for tap in range(FFN_CONV_W - 1):
        off = HALO - (FFN_CONV_W - 1) + tap
        acc = acc + cw_ref[tap:tap + 1, :] * gp_ref[off:off + tm, :]
    o_ref[...] = (_silu(acc) * up).astype(o_ref.dtype)


def ffn_up(x, gain, wg, wu, conv_w, conv_b, *, bsz, seq, tm, tn):
    t, d = x.shape
    f = wg.shape[1]
    assert seq % tm == 0 and f % tn == 0
    ni, nj = seq // tm, f // tn
    return pl.pallas_call(
        _ffn_up_kernel,
        grid=(bsz, ni, nj),
        in_specs=[
            pl.BlockSpec((tm, d), lambda b, i, j: (b * ni + i, 0)),
            pl.BlockSpec((1, d), lambda b, i, j: (0, 0)),
            pl.BlockSpec((d, tn), lambda b, i, j: (0, j)),
            pl.BlockSpec((d, tn), lambda b, i, j: (0, j)),
            pl.BlockSpec((FFN_CONV_W, tn), lambda b, i, j: (0, j)),
            pl.BlockSpec((1, tn), lambda b, i, j: (0, j)),
        ],
        out_specs=pl.BlockSpec((tm, tn), lambda b, i, j: (b * ni + i, j)),
        out_shape=jax.ShapeDtypeStruct((t, f), BF16),
        scratch_shapes=[
            pltpu.VMEM((tm, d), BF16),
            pltpu.VMEM((tm + HALO, tn), F32),
            pltpu.VMEM((nj, HALO, tn), F32),
        ],
        compiler_params=_cparams(("parallel", "arbitrary", "arbitrary")),
        name="ffn_up",
    )(x, gain.reshape(1, d), wg, wu, conv_w, conv_b.reshape(1, f))


def _inproj_kernel(x_ref, g_ref, w_ref, cw_ref, cb_ref, ws_ref, o_ref, os_ref, xn_ref, gp_ref, halo_ref):
    i = pl.program_id(1)
    j = pl.program_id(2)
    tm = x_ref.shape[0]

    @pl.when(j == 0)
    def _():
        x = x_ref[...]
        xn = (x * _rms_scale(x) * g_ref[...]).astype(BF16)
        xn_ref[...] = xn
        os_ref[...] = _dot(xn, ws_ref[...])

    @pl.when(i == 0)
    def _():
        gp_ref[0:HALO, :] = jnp.zeros((HALO, gp_ref.shape[1]), F32)

    @pl.when(i > 0)
    def _():
        gp_ref[0:HALO, :] = halo_ref[j]

    acc = _dot(xn_ref[...], w_ref[...])
    gp_ref[HALO:HALO + tm, :] = acc
    halo_ref[j] = acc[tm - HALO:tm, :]
    conv = cb_ref[...] + cw_ref[CONV_W - 1:CONV_W, :] * acc
    for tap in range(CONV_W - 1):
        off = HALO - (CONV_W - 1) + tap
        conv = conv + cw_ref[tap:tap + 1, :] * gp_ref[off:off + tm, :]
    o_ref[...] = _silu(conv).astype(BF16)


def mixer_in_proj(x, gain, w, w_small, conv_w, conv_b, *, bsz, seq, tm, tn):
    t, d = x.shape
    n = w.shape[1]
    conv_dim = conv_w.shape[1]
    ns = w_small.shape[1]
    assert seq % tm == 0 and n % tn == 0
    ni, nj = seq // tm, n // tn
    identity = jnp.zeros((CONV_W, n - conv_dim), F32).at[CONV_W - 1].set(1.0)
    taps = jnp.concatenate([conv_w.astype(F32), identity], axis=1)
    bias = jnp.pad(conv_b.astype(F32), (0, n - conv_dim)).reshape(1, n)
    return pl.pallas_call(
        _inproj_kernel,
        grid=(bsz, ni, nj),
        in_specs=[
            pl.BlockSpec((tm, d), lambda b, i, j: (b * ni + i, 0)),
            pl.BlockSpec((1, d), lambda b, i, j: (0, 0)),
            pl.BlockSpec((d, tn), lambda b, i, j: (0, j)),
            pl.BlockSpec((CONV_W, tn), lambda b, i, j: (0, j)),
            pl.BlockSpec((1, tn), lambda b, i, j: (0, j)),
            pl.BlockSpec((d, ns), lambda b, i, j: (0, 0)),
        ],
        out_specs=[
            pl.BlockSpec((tm, tn), lambda b, i, j: (b * ni + i, j)),
            pl.BlockSpec((tm, ns), lambda b, i, j: (b * ni + i, 0)),
        ],
        out_shape=[jax.ShapeDtypeStruct((t, n), BF16), jax.ShapeDtypeStruct((t, ns), F32)],
        scratch_shapes=[
            pltpu.VMEM((tm, d), BF16),
            pltpu.VMEM((tm + HALO, tn), F32),
            pltpu.VMEM((nj, HALO, tn), F32),
        ],
        compiler_params=_cparams(("parallel", "arbitrary", "arbitrary")),
        name="mixer_in_proj",
    )(x, gain.reshape(1, d), w, taps, bias, w_small)


def _split2(x):
    hi = x.astype(BF16)
    return hi, (x - hi.astype(F32)).astype(BF16)


def _select_many(splits, sel):
    rows = [term for parts in splits for term in parts]
    q = rows[0].shape[0]
    full = _dot(jnp.concatenate(rows, axis=0), sel)
    outs, i = [], 0
    for parts in splits:
        acc = full[i * q:(i + 1) * q, :]
        for k in range(1, len(parts)):
            acc = acc + full[(i + k) * q:(i + k + 1) * q, :]
        outs.append(acc)
        i += len(parts)
    return outs


def _select(parts, sel):
    acc = _dot(parts[0], sel)
    for term in parts[1:]:
        acc = acc + _dot(term, sel)
    return acc


def _ssd_kernel(xbc_ref, zs_ref, dt_ref, dtb_ref, alog_ref, dskip_ref, nw_ref,
                ehead_ref, ecol_ref, o_ref, s_ref, *, q, groups, inner, state):
    c = pl.program_id(1)
    gw = inner // groups
    hpg = gw // M_HEAD_DIM

    @pl.when(c == 0)
    def _():
        s_ref[...] = jnp.zeros_like(s_ref)

    dt = _softplus(dt_ref[...] + dtb_ref[...])
    a_neg = -jnp.exp(alog_ref[...])
    tril = _tril_mask(q)
    acum = _dot_sel_lhs(_as_sel(tril), dt * a_neg)
    acum_t = acum.T
    dt3 = _split3(dt)
    ac3 = _split3(acum)
    lane_head = lax.broadcasted_iota(jnp.int32, (q, gw), 1) // M_HEAD_DIM
    head_sel = [jnp.where(lane_head == hh, 1.0, 0.0).astype(BF16) for hh in range(hpg)]

    for g in range(groups):
        xs = xbc_ref[:, g * gw:(g + 1) * gw].astype(F32)
        bm16 = xbc_ref[:, inner + g * state:inner + (g + 1) * state]
        c_lo = inner + groups * state + g * state
        cm16 = xbc_ref[:, c_lo:c_lo + state]
        e_g = ehead_ref[:, g * gw:(g + 1) * gw]
        dt_x = _select(dt3, e_g)
        ac_x = _select(ac3, e_g)
        ac_col = _select(ac3, ecol_ref[:, g * hpg * q:(g + 1) * hpg * q])
        xdt = xs * dt_x
        xdt16 = xdt.astype(BF16)
        cb = _dot_nt(cm16, bm16)
        lhs, rhs = [], []
        for hh in range(hpg):
            h = g * hpg + hh
            diff = ac_col[:, hh * q:(hh + 1) * q] - acum_t[h:h + 1, :]
            dec = jnp.exp(jnp.where(tril, diff, NEG_INF))
            lhs.append((cb * dec).astype(BF16))
            rhs.append(xdt16 * head_sel[hh])
        y = _dot(jnp.concatenate(lhs, axis=1), jnp.concatenate(rhs, axis=0))
        s_old = s_ref[g]
        y = y + _dot(cm16, s_old.astype(BF16)) * jnp.exp(ac_x)
        y = y + dskip_ref[:, g * gw:(g + 1) * gw] * xs
        last = ac_x[q - 1:q, :]
        xw = (xdt * jnp.exp(last - ac_x)).astype(BF16)
        s_ref[g] = s_old * jnp.exp(last) + _dot_tn(bm16, xw)
        y = y * zs_ref[:, g * gw:(g + 1) * gw].astype(F32)
        o_ref[:, g * gw:(g + 1) * gw] = (y * _rms_scale(y) * nw_ref[:, g * gw:(g + 1) * gw]).astype(BF16)


def ssd_mixer_core(act, dt_raw, dt_bias, a_log, d_skip, norm_w, *, bsz, seq, inner, heads):
    t = act.shape[0]
    q = SSD_CHUNK
    nc = seq // q
    conv_dim = inner + 2 * M_GROUPS * M_STATE
    assert conv_dim == 2 * inner and heads <= LANES_V7X
    pad = LANES_V7X - heads
    head_ids = jnp.arange(LANES_V7X)
    ehead = (head_ids[:, None] == (jnp.arange(inner) // M_HEAD_DIM)[None, :]).astype(BF16)
    ecol = (head_ids[:, None] == (jnp.arange(heads * q) // q)[None, :]).astype(BF16)
    z_blk = conv_dim // inner
    row = lambda b, c: b * nc + c
    const = lambda shape: pl.BlockSpec(shape, lambda b, c: (0, 0))
    return pl.pallas_call(
        functools.partial(_ssd_kernel, q=q, groups=M_GROUPS, inner=inner, state=M_STATE),
        grid=(bsz, nc),
        in_specs=[
            pl.BlockSpec((q, conv_dim), lambda b, c: (row(b, c), 0)),
            pl.BlockSpec((q, inner), lambda b, c: (row(b, c), z_blk)),
            pl.BlockSpec((q, LANES_V7X), lambda b, c: (row(b, c), 0)),
            const((1, LANES_V7X)),
            const((1, LANES_V7X)),
            const((1, inner)),
            const((1, inner)),
            const((LANES_V7X, inner)),
            const((LANES_V7X, heads * q)),
        ],
        out_specs=pl.BlockSpec((q, inner), lambda b, c: (row(b, c), 0)),
        out_shape=jax.ShapeDtypeStruct((t, inner), BF16),
        scratch_shapes=[pltpu.VMEM((M_GROUPS, M_STATE, inner // M_GROUPS), F32)],
        compiler_params=_cparams(("parallel", "arbitrary")),
        name="ssd_core",
    )(act, act, dt_raw,
      jnp.pad(dt_bias, (0, pad)).reshape(1, LANES_V7X), jnp.pad(a_log, (0, pad)).reshape(1, LANES_V7X),
      jnp.repeat(d_skip, M_HEAD_DIM).reshape(1, inner), norm_w.reshape(1, inner), ehead, ecol)


def _hgrn2_kernel(q_ref, f_ref, i_ref, g_ref, lb_ref, nw_ref, o_ref, s_ref, *, q, heads):
    c = pl.program_id(1)
    dk = H_EXPAND

    @pl.when(c == 0)
    def _():
        s_ref[...] = jnp.zeros_like(s_ref)

    lb = lb_ref[...]
    forget = lb + (1.0 - lb) * jax.nn.sigmoid(f_ref[...])
    tril = _tril_mask(q)
    gc = _dot_sel_lhs(_as_sel(tril), jnp.log(forget))
    key = 1.0 - forget
    qs = _silu(q_ref[...]) * (dk ** -0.5)
    mid = q // 2 - 1
    g_mid = gc[mid:mid + 1, :]
    g_last = gc[q - 1:q, :]
    q_dec = (qs * jnp.exp(gc - g_mid)).astype(BF16)
    k_inv = (key * jnp.exp(g_mid - gc)).astype(BF16)
    q_in = (qs * jnp.exp(gc)).astype(BF16)
    k_end = (key * jnp.exp(g_last - gc)).astype(BF16)
    e_last = jnp.exp(g_last)
    for h in range(heads):
        sl = slice(h * dk, (h + 1) * dk)
        v_h = i_ref[:, sl].astype(BF16)
        att = jnp.where(tril, _dot_nt(q_dec[:, sl], k_inv[:, sl]), 0.0)
        st_old = s_ref[h]
        o_h = _dot(att.astype(BF16), v_h) + _dot_nt(q_in[:, sl], st_old.astype(BF16))
        s_ref[h] = st_old * e_last[:, sl] + _dot_tn(v_h, k_end[:, sl])
        o_h = o_h * _rms_scale(o_h) * nw_ref[...]
        o_ref[:, sl] = (o_h * _silu(g_ref[:, sl])).astype(BF16)


def hgrn2_mixer_core(proj, lower_bound, norm_w, *, bsz, seq, d):
    t = proj.shape[0]
    q = HGRN_CHUNK
    nc = seq // q
    heads = d // H_EXPAND
    row = lambda b, c: b * nc + c
    part = lambda k: pl.BlockSpec((q, d), lambda b, c: (row(b, c), k))
    return pl.pallas_call(
        functools.partial(_hgrn2_kernel, q=q, heads=heads),
        grid=(bsz, nc),
        in_specs=[part(0), part(1), part(2), part(3),
                  pl.BlockSpec((1, d), lambda b, c: (0, 0)),
                  pl.BlockSpec((1, H_EXPAND), lambda b, c: (0, 0))],
        out_specs=pl.BlockSpec((q, d), lambda b, c: (row(b, c), 0)),
        out_shape=jax.ShapeDtypeStruct((t, d), BF16),
        scratch_shapes=[pltpu.VMEM((heads, H_EXPAND, d // heads), F32)],
        compiler_params=_cparams(("parallel", "arbitrary")),
        name="hgrn2_core",
    )(proj, proj, proj, proj, lower_bound.reshape(1, d), norm_w.reshape(1, H_EXPAND))


def _block_diag(blocks):
    n = len(blocks)
    rows = []
    for e, blk in enumerate(blocks):
        zero = jnp.zeros_like(blk)
        rows.append(jnp.concatenate([blk if k == e else zero for k in range(n)], axis=1))
    return jnp.concatenate(rows, axis=0)


def _gdn_kernel(qkv_ref, zs_ref, ba_ref, dtb_ref, alog_ref, nw_ref, e64_ref, e128_ref,
                o_ref, s_ref, *, q, qk_heads, v_heads):
    c = pl.program_id(1)
    dh = G_HEAD_DIM
    key_dim = qk_heads * dh
    rep = v_heads // qk_heads

    @pl.when(c == 0)
    def _():
        s_ref[...] = jnp.zeros_like(s_ref)

    tril = _tril_mask(q)
    beta = jax.nn.sigmoid(ba_ref[:, 0:LANES_V7X])
    gate = -jnp.exp(alog_ref[...]) * _softplus(ba_ref[:, LANES_V7X:2 * LANES_V7X] + dtb_ref[...])
    gc = _dot_sel_lhs(_as_sel(tril), gate)
    gc3 = _split3(gc)
    beta3 = _split3(beta)

    eye_t = (lax.broadcasted_iota(jnp.int32, (q, v_heads * q), 0)
             == lax.broadcasted_iota(jnp.int32, (q, v_heads * q), 1) % q)
    col_s = lax.broadcasted_iota(jnp.int32, (q, v_heads * q), 1) % q
    row_l = lax.broadcasted_iota(jnp.int32, (q, v_heads * q), 0)
    g_col, b_col = _select_many([gc3, beta3], e64_ref[...])
    g_row = jnp.sum(jnp.where(eye_t, g_col, 0.0), axis=0, keepdims=True)
    decay = jnp.exp(jnp.where(row_l >= col_s, g_col - g_row, NEG_INF))

    kk_parts, qk_parts, k_heads, q_heads = [], [], [], []
    for p in range(qk_heads):
        q_p = qkv_ref[:, p * dh:(p + 1) * dh].astype(F32)
        k_p = qkv_ref[:, key_dim + p * dh:key_dim + (p + 1) * dh].astype(F32)
        q_p = q_p * lax.rsqrt(jnp.sum(q_p * q_p, axis=-1, keepdims=True) + EPS) * (dh ** -0.5)
        k_p = k_p * lax.rsqrt(jnp.sum(k_p * k_p, axis=-1, keepdims=True) + EPS)
        k16 = k_p.astype(BF16)
        k_rep = jnp.concatenate([k16] * rep, axis=0)
        kk_parts.append(_dot_nt(k16, k_rep))
        qk_parts.append(_dot_nt(q_p.astype(BF16), k_rep))
        k_heads.append(k_p)
        q_heads.append(q_p)
    kk = jnp.concatenate(kk_parts, axis=1)
    qk = jnp.concatenate(qk_parts, axis=1)
    m_all = jnp.where(row_l > col_s, b_col * kk * decay, 0.0)
    att_all = qk * decay

    quad_w = GDN_QUAD * q
    n_quads = v_heads // GDN_QUAD
    lane_blk = lax.broadcasted_iota(jnp.int32, (q, quad_w), 1) // q
    blk_sel = [jnp.where(lane_blk == e, 1.0, 0.0).astype(BF16) for e in range(GDN_QUAD)]

    def blocks_on_diagonal(x16):
        return jnp.concatenate([x16 * blk_sel[e] for e in range(GDN_QUAD)], axis=0)

    def times_blockwise(lhs_hi, lhs_lo, bd_hi, bd_lo):
        k = len(lhs_hi)
        full = _dot(jnp.concatenate(lhs_hi + lhs_lo, axis=0), bd_hi)
        part = _dot(jnp.concatenate(lhs_hi, axis=0), bd_lo)
        return [full[i * q:(i + 1) * q, :] + full[(k + i) * q:(k + i + 1) * q, :] + part[i * q:(i + 1) * q, :]
                for i in range(k)]

    levels = q.bit_length() - 2
    p_cur = [-m_all[:, u * quad_w:(u + 1) * quad_w] for u in range(n_quads)]
    n_acc = list(p_cur)
    for lvl in range(levels + 1):
        for u in range(n_quads):
            p_hi, p_lo = _split2(p_cur[u])
            bd_hi, bd_lo = blocks_on_diagonal(p_hi), blocks_on_diagonal(p_lo)
            if lvl == 0:
                (p_cur[u],) = times_blockwise([p_hi], [p_lo], bd_hi, bd_lo)
            elif lvl < levels:
                n_hi, n_lo = _split2(n_acc[u])
                n_p, p_sq = times_blockwise([n_hi, p_hi], [n_lo, p_lo], bd_hi, bd_lo)
                n_acc[u] = n_acc[u] + p_cur[u] + n_p
                p_cur[u] = p_sq
            else:
                n_hi, n_lo = _split2(n_acc[u])
                (n_p,) = times_blockwise([n_hi], [n_lo], bd_hi, bd_lo)
                n_acc[u] = n_acc[u] + p_cur[u] + n_p

    g_x_all, b_x_all = _select_many([gc3, beta3], e128_ref[...])
    for u in range(n_quads):
        att_u = att_all[:, u * quad_w:(u + 1) * quad_w]
        vb, kbg, qd, ke, el = [], [], [], [], []
        for e in range(GDN_QUAD):
            h = u * GDN_QUAD + e
            p = h // rep
            g_x = g_x_all[:, h * dh:(h + 1) * dh]
            b_x = b_x_all[:, h * dh:(h + 1) * dh]
            eg = jnp.exp(g_x)
            g_l = g_x[q - 1:q, :]
            v_h = qkv_ref[:, 2 * key_dim + h * dh:2 * key_dim + (h + 1) * dh].astype(F32)
            vb.append(v_h * b_x)
            kbg.append(k_heads[p] * b_x * eg)
            qd.append(q_heads[p] * eg)
            ke.append(k_heads[p] * jnp.exp(g_l - g_x))
            el.append(jnp.exp(g_l))
        rhs = _block_diag([jnp.concatenate([vb[e], kbg[e]], axis=1).astype(BF16) for e in range(GDN_QUAD)])
        sol = _dot(n_acc[u].astype(BF16), rhs)
        v_new = []
        o_inter = []
        for e in range(GDN_QUAD):
            h = u * GDN_QUAD + e
            u_h = vb[e] + sol[:, e * 2 * dh:e * 2 * dh + dh]
            w_h = kbg[e] + sol[:, e * 2 * dh + dh:(e + 1) * 2 * dh]
            s16 = s_ref[h].astype(BF16)
            both = _dot(jnp.concatenate([w_h, qd[e]], axis=0).astype(BF16), s16)
            v_new.append(u_h - both[0:q, :])
            o_inter.append(both[q:2 * q, :])
        o_intra = _dot(att_u.astype(BF16), _block_diag([v.astype(BF16) for v in v_new]))
        for e in range(GDN_QUAD):
            h = u * GDN_QUAD + e
            sl = slice(h * dh, (h + 1) * dh)
            s_ref[h] = s_ref[h] * el[e] + _dot_tn(ke[e].astype(BF16), v_new[e].astype(BF16))
            o_h = o_inter[e] + o_intra[:, e * dh:(e + 1) * dh]
            o_h = o_h * _rms_scale(o_h) * nw_ref[...]
            o_ref[:, sl] = (o_h * zs_ref[:, sl].astype(F32)).astype(BF16)


def gdn_mixer_core(act, ba_raw, a_log, dt_bias, norm_w, *, bsz, seq, qk_heads, v_heads):
    t = act.shape[0]
    q = GDN_CHUNK
    nc = seq // q
    dh = G_HEAD_DIM
    key_dim = qk_heads * dh
    val_dim = v_heads * dh
    conv_dim = 2 * key_dim + val_dim
    assert conv_dim == 2 * val_dim and v_heads % GDN_QUAD == 0
    pad = LANES_V7X - v_heads
    head_ids = jnp.arange(LANES_V7X)
    e64 = (head_ids[:, None] == (jnp.arange(v_heads * q) // q)[None, :]).astype(BF16)
    e128 = (head_ids[:, None] == (jnp.arange(val_dim) // dh)[None, :]).astype(BF16)
    z_blk = conv_dim // val_dim
    row = lambda b, c: b * nc + c
    const = lambda shape: pl.BlockSpec(shape, lambda b, c: (0, 0))
    return pl.pallas_call(
        functools.partial(_gdn_kernel, q=q, qk_heads=qk_heads, v_heads=v_heads),
        grid=(bsz, nc),
        in_specs=[
            pl.BlockSpec((q, conv_dim), lambda b, c: (row(b, c), 0)),
            pl.BlockSpec((q, val_dim), lambda b, c: (row(b, c), z_blk)),
            pl.BlockSpec((q, 2 * LANES_V7X), lambda b, c: (row(b, c), 0)),
            const((1, LANES_V7X)),
            const((1, LANES_V7X)),
            const((1, dh)),
            const((LANES_V7X, v_heads * q)),
            const((LANES_V7X, val_dim)),
        ],
        out_specs=pl.BlockSpec((q, val_dim), lambda b, c: (row(b, c), 0)),
        out_shape=jax.ShapeDtypeStruct((t, val_dim), BF16),
        scratch_shapes=[pltpu.VMEM((v_heads, dh, dh), F32)],
        compiler_params=_cparams(("parallel", "arbitrary")),
        name="gdn_core",
    )(act, act, ba_raw, jnp.pad(dt_bias, (0, pad)).reshape(1, LANES_V7X),
      jnp.pad(a_log, (0, pad)).reshape(1, LANES_V7X), norm_w.reshape(1, dh), e64, e128)


def _pad_cols(w, width):
    return jnp.pad(w, ((0, 0), (0, width - w.shape[1])))


def _ssd_in_weights(in_w, inner, conv_dim):
    z = in_w[:, :inner]
    xbc = in_w[:, inner:inner + conv_dim]
    dt = _pad_cols(in_w[:, inner + conv_dim:], LANES_V7X)
    return jnp.concatenate([xbc, z], axis=1).astype(BF16), dt.astype(BF16)


def _gdn_in_weights(in_w, conv_dim, val_dim, v_heads):
    main = in_w[:, :conv_dim + val_dim]
    b = _pad_cols(in_w[:, conv_dim + val_dim:conv_dim + val_dim + v_heads], LANES_V7X)
    a = _pad_cols(in_w[:, conv_dim + val_dim + v_heads:], LANES_V7X)
    return main.astype(BF16), jnp.concatenate([b, a], axis=1).astype(BF16)


def _pick_tile(n, candidates):
    for cand in candidates:
        if n % cand == 0:
            return cand
    raise ValueError(f"no tile for {n} in {candidates}")


def kernel(x, mem, ln_mix, ln_xattn, ln_mem, ln_ffn, final_norm, m_in_w, m_conv_w, m_conv_b, m_dt_bias, m_a_log, m_d, m_norm_w, m_out_w, h_in_w, h_lower_bounds, h_norm_w, h_out_w, g_in_w, g_conv_w, g_a_log, g_dt_bias, g_norm_w, g_out_w, xa_q, xa_kv, xa_o, f_up, f_conv_w, f_conv_b, f_down):
    bsz, seq, d = x.shape
    n_mem = mem.shape[1]
    depth = ln_mix.shape[0]
    t = bsz * seq
    tm = _pick_tile(t, (1024, 512, 256, 128, 64))
    tq = _pick_tile(seq, (512, 256, 128, 64))
    ts = _pick_tile(seq, (1024, 512, 256, 128, 64))

    lb = jnp.cumsum(jax.nn.softmax(h_lower_bounds.astype(F32), axis=0), axis=0)
    lb = lb - lb[:1]

    m_inner = m_out_w.shape[1]
    m_heads = m_dt_bias.shape[1]
    m_conv_dim = m_conv_w.shape[2]
    g_v_heads = g_a_log.shape[1]
    g_val_dim = g_out_w.shape[1]
    g_conv_dim = g_conv_w.shape[2]
    g_qk_heads = (g_conv_dim - g_val_dim) // (2 * G_HEAD_DIM)
    d_ff = f_down.shape[1]

    xf = x.reshape(t, d)
    memf = mem.reshape(bsz * n_mem, d)
    ia = ib = ic = 0
    for i in range(depth):
        kind = i % N_MIXERS
        if kind == 0:
            w_in, w_dt = _ssd_in_weights(m_in_w[ia], m_inner, m_conv_dim)
            act, dt_raw = mixer_in_proj(xf, ln_mix[i], w_in, w_dt, m_conv_w[ia], m_conv_b[ia],
                                        bsz=bsz, seq=seq, tm=ts, tn=1024)
            y = ssd_mixer_core(act, dt_raw, m_dt_bias[ia], m_a_log[ia], m_d[ia],
                               m_norm_w[ia], bsz=bsz, seq=seq, inner=m_inner, heads=m_heads)
            xf = matmul_residual(y, m_out_w[ia].astype(BF16), xf, tm=tq)
            ia += 1
        elif kind == 1:
            proj = norm_matmul(xf, ln_mix[i], h_in_w[ib].astype(BF16), tm=tm, tn=1024)
            y = hgrn2_mixer_core(proj, lb[i], h_norm_w[ib], bsz=bsz, seq=seq, d=d)
            xf = matmul_residual(y, h_out_w[ib].astype(BF16), xf, tm=tq)
            ib += 1
        else:
            w_in, w_ba = _gdn_in_weights(g_in_w[ic], g_conv_dim, g_val_dim, g_v_heads)
            act, ba_raw = mixer_in_proj(xf, ln_mix[i], w_in, w_ba, g_conv_w[ic], jnp.zeros((g_conv_dim,), F32),
                                        bsz=bsz, seq=seq, tm=ts, tn=1024)
            y = gdn_mixer_core(act, ba_raw, g_a_log[ic], g_dt_bias[ic], g_norm_w[ic],
                               bsz=bsz, seq=seq, qk_heads=g_qk_heads, v_heads=g_v_heads)
            xf = matmul_residual(y, g_out_w[ic].astype(BF16), xf, tm=tq)
            ic += 1

        kv = norm_matmul(memf, ln_mem[i], xa_kv[i].astype(BF16), tm=_pick_tile(bsz * n_mem, (1024, 512, 256)),
                         tn=1024, out_dtype=BF16)
        xf = memory_cross_attention(xf, ln_xattn[i], xa_q[i].astype(BF16), kv, xa_o[i].astype(BF16),
                                    bsz=bsz, seq=seq, n_mem=n_mem, tq=tq)

        up_w = f_up[i].astype(BF16)
        act = ffn_up(xf, ln_ffn[i], up_w[:, :d_ff], up_w[:, d_ff:], f_conv_w[i], f_conv_b[i],
                     bsz=bsz, seq=seq, tm=tq, tn=_pick_tile(d_ff, (1408, 256, 128)))
        last = i == depth - 1
        xf = matmul_residual(act, f_down[i].astype(BF16), xf, tm=tq,
                             final_gain=final_norm if last else None)
    return xf.reshape(bsz, seq, d)
```

```python
import functools

import jax
import jax.numpy as jnp
from jax import lax
from jax.experimental import pallas as pl
from jax.experimental.pallas import tpu as pltpu

F32 = jnp.float32
BF16 = jnp.bfloat16
EPS = 1e-6
NEG_INF = float("-inf")

LANES_V7X = 128
SUBLANES_V7X = 8
VMEM_LIMIT_BYTES_V7X = 56 * 1024 * 1024

CONV_W = 4
FFN_CONV_W = 3
HALO = SUBLANES_V7X

M_HEAD_DIM = 64
M_GROUPS = 8
M_STATE = 128
SSD_CHUNK = 128
H_EXPAND = 128
HGRN_CHUNK = 64
G_HEAD_DIM = 128
GDN_CHUNK = 64
GDN_QUAD = 4
X_HEADS = 4
N_MIXERS = 3


def _cparams(sem):
    return pltpu.CompilerParams(dimension_semantics=sem, vmem_limit_bytes=VMEM_LIMIT_BYTES_V7X)


def _dot(a, b):
    return jnp.dot(a, b, preferred_element_type=F32)


def _dot_nt(a, b):
    return lax.dot_general(a, b, (((1,), (1,)), ((), ())), preferred_element_type=F32)


def _dot_tn(a, b):
    return lax.dot_general(a, b, (((0,), (0,)), ((), ())), preferred_element_type=F32)


def _split3(x):
    hi = x.astype(BF16)
    r1 = x - hi.astype(F32)
    mid = r1.astype(BF16)
    lo = (r1 - mid.astype(F32)).astype(BF16)
    return hi, mid, lo


def _dot_sel_lhs(sel, x):
    parts = _split3(x)
    return _dot(jnp.concatenate([sel] * len(parts), axis=1), jnp.concatenate(parts, axis=0))


def _silu(x):
    return x * jax.nn.sigmoid(x)


def _softplus(x):
    return jnp.maximum(x, 0.0) + jnp.log1p(jnp.exp(-jnp.abs(x)))


def _rms_scale(x):
    return lax.rsqrt(jnp.mean(x * x, axis=-1, keepdims=True) + EPS)


def _tril_mask(n):
    row = lax.broadcasted_iota(jnp.int32, (n, n), 0)
    col = lax.broadcasted_iota(jnp.int32, (n, n), 1)
    return row >= col


def _as_sel(mask):
    return jnp.where(mask, 1.0, 0.0).astype(BF16)


def _shift_rows_down(x, history, s):
    rows, n = x.shape
    groups = x.reshape(rows // SUBLANES_V7X, SUBLANES_V7X, n)
    rolled = pltpu.roll(groups, s, axis=1)
    hist_rolled = pltpu.roll(history, s, axis=0).reshape(1, SUBLANES_V7X, n)
    prev = jnp.concatenate([hist_rolled, rolled[:-1]], axis=0)
    sub = lax.broadcasted_iota(jnp.int32, rolled.shape, 1)
    return jnp.where(sub < s, prev, rolled).reshape(rows, n)


def _norm_matmul_kernel(x_ref, g_ref, w_ref, o_ref, xn_ref):
    @pl.when(pl.program_id(1) == 0)
    def _():
        x = x_ref[...]
        xn_ref[...] = (x * _rms_scale(x) * g_ref[...]).astype(BF16)

    o_ref[...] = _dot(xn_ref[...], w_ref[...]).astype(o_ref.dtype)


def norm_matmul(x, gain, w, *, tm, tn, out_dtype=F32):
    t, d = x.shape
    n = w.shape[1]
    assert t % tm == 0 and n % tn == 0, (t, tm, n, tn)
    return pl.pallas_call(
        _norm_matmul_kernel,
        grid=(t // tm, n // tn),
        in_specs=[
            pl.BlockSpec((tm, d), lambda i, j: (i, 0)),
            pl.BlockSpec((1, d), lambda i, j: (0, 0)),
            pl.BlockSpec((d, tn), lambda i, j: (0, j)),
        ],
        out_specs=pl.BlockSpec((tm, tn), lambda i, j: (i, j)),
        out_shape=jax.ShapeDtypeStruct((t, n), out_dtype),
        scratch_shapes=[pltpu.VMEM((tm, d), BF16)],
        compiler_params=_cparams(("parallel", "arbitrary")),
        name="norm_matmul",
    )(x, gain.reshape(1, d), w)


def _matmul_res_kernel(a_ref, w_ref, r_ref, o_ref):
    o_ref[...] = r_ref[...] + _dot(a_ref[...], w_ref[...])


def matmul_residual(a, w, res, *, tm):
    t, k = a.shape
    d = w.shape[1]
    assert t % tm == 0
    return pl.pallas_call(
        _matmul_res_kernel,
        grid=(t // tm,),
        in_specs=[
            pl.BlockSpec((tm, k), lambda i: (i, 0)),
            pl.BlockSpec((k, d), lambda i: (0, 0)),
            pl.BlockSpec((tm, d), lambda i: (i, 0)),
        ],
        out_specs=pl.BlockSpec((tm, d), lambda i: (i, 0)),
        out_shape=jax.ShapeDtypeStruct((t, d), F32),
        compiler_params=_cparams(("parallel",)),
        name="matmul_residual",
    )(a, w, res)


def _xattn_kernel(x_ref, g_ref, wq_ref, kv_ref, wo_ref, o_ref, *, heads):
    x = x_ref[...]
    d = x.shape[1]
    dh = d // heads
    xn = (x * _rms_scale(x) * g_ref[...]).astype(BF16)
    q = (_dot(xn, wq_ref[...]) * (dh ** -0.5)).astype(BF16)
    outs = []
    for h in range(heads):
        k_h = kv_ref[:, h * dh:(h + 1) * dh]
        v_h = kv_ref[:, d + h * dh:d + (h + 1) * dh]
        s = _dot_nt(q[:, h * dh:(h + 1) * dh], k_h)
        p = jnp.exp(s - jnp.max(s, axis=-1, keepdims=True))
        o_h = _dot(p.astype(BF16), v_h) / jnp.sum(p, axis=-1, keepdims=True)
        outs.append(o_h.astype(BF16))
    o_ref[...] = x + _dot(jnp.concatenate(outs, axis=1), wo_ref[...])


def memory_cross_attention(x, gain, wq, kv, wo, *, bsz, seq, n_mem, tq):
    t, d = x.shape
    nq = seq // tq
    return pl.pallas_call(
        functools.partial(_xattn_kernel, heads=X_HEADS),
        grid=(bsz, nq),
        in_specs=[
            pl.BlockSpec((tq, d), lambda b, i: (b * nq + i, 0)),
            pl.BlockSpec((1, d), lambda b, i: (0, 0)),
            pl.BlockSpec((d, d), lambda b, i: (0, 0)),
            pl.BlockSpec((n_mem, 2 * d), lambda b, i: (b, 0)),
            pl.BlockSpec((d, d), lambda b, i: (0, 0)),
        ],
        out_specs=pl.BlockSpec((tq, d), lambda b, i: (b * nq + i, 0)),
        out_shape=jax.ShapeDtypeStruct((t, d), F32),
        compiler_params=_cparams(("parallel", "parallel")),
        name="memory_xattn",
    )(x, gain.reshape(1, d), wq, kv, wo)


def _ffn_kernel(x_ref, g_ref, wg_ref, wu_ref, cw_ref, cb_ref, wd_ref, *rest, final_norm):
    if final_norm:
        fg_ref, o_ref, act_ref, halo_ref = rest
    else:
        o_ref, act_ref, halo_ref = rest
    tm = x_ref.shape[0]
    f = wg_ref.shape[1]

    @pl.when(pl.program_id(1) == 0)
    def _():
        halo_ref[...] = jnp.zeros_like(halo_ref)

    x = x_ref[...]
    xn = (x * _rms_scale(x) * g_ref[...]).astype(BF16)
    cols_per = 2 * LANES_V7X
    for c in range(0, f, cols_per):
        cols = slice(c, c + cols_per)
        gate = _dot(xn, wg_ref[:, cols])
        up = _dot(xn, wu_ref[:, cols])
        history = halo_ref[:, cols]
        halo_ref[:, cols] = gate[tm - HALO:tm, :]
        conv = cb_ref[:, cols] + cw_ref[FFN_CONV_W - 1:FFN_CONV_W, cols] * gate
        for tap in range(FFN_CONV_W - 1):
            conv = conv + cw_ref[tap:tap + 1, cols] * _shift_rows_down(gate, history, FFN_CONV_W - 1 - tap)
        act_ref[:, cols] = (_silu(conv) * up).astype(BF16)
    y = x + _dot(act_ref[...], wd_ref[...])
    if final_norm:
        y = y * _rms_scale(y) * fg_ref[...]
    o_ref[...] = y


def conv_glu_ffn(x, gain, wg, wu, conv_w, conv_b, wd, *, bsz, seq, tm, final_gain=None):
    t, d = x.shape
    f = wg.shape[1]
    assert seq % tm == 0 and f % (2 * LANES_V7X) == 0
    ni = seq // tm
    resident = lambda shape: pl.BlockSpec(shape, lambda b, i: (0, 0), pipeline_mode=pl.Buffered(1))
    in_specs = [
        pl.BlockSpec((tm, d), lambda b, i: (b * ni + i, 0)),
        resident((1, d)),
        resident((d, f)),
        resident((d, f)),
        resident((FFN_CONV_W, f)),
        resident((1, f)),
        resident((f, d)),
    ]
    args = [x, gain.reshape(1, d), wg, wu, conv_w, conv_b.reshape(1, f), wd]
    if final_gain is not None:
        in_specs.append(resident((1, d)))
        args.append(final_gain.reshape(1, d))
    return pl.pallas_call(
        functools.partial(_ffn_kernel, final_norm=final_gain is not None),
        grid=(bsz, ni),
        in_specs=in_specs,
        out_specs=pl.BlockSpec((tm, d), lambda b, i: (b * ni + i, 0)),
        out_shape=jax.ShapeDtypeStruct((t, d), F32),
        scratch_shapes=[pltpu.VMEM((tm, f), BF16), pltpu.VMEM((HALO, f), F32)],
        compiler_params=_cparams(("parallel", "arbitrary")),
        name="conv_glu_ffn",
    )(*args)


def _inproj_kernel(x_ref, g_ref, w_ref, cw_ref, cb_ref, ws_ref, o_ref, os_ref, xn_ref, gp_ref, halo_ref):
    i = pl.program_id(1)
    j = pl.program_id(2)
    tm = x_ref.shape[0]

    @pl.when(j == 0)
    def _():
        x = x_ref[...]
        xn = (x * _rms_scale(x) * g_ref[...]).astype(BF16)
        xn_ref[...] = xn
        os_ref[...] = _dot(xn, ws_ref[...])

    @pl.when(i == 0)
    def _():
        gp_ref[0:HALO, :] = jnp.zeros((HALO, gp_ref.shape[1]), F32)

    @pl.when(i > 0)
    def _():
        gp_ref[0:HALO, :] = halo_ref[j]

    acc = _dot(xn_ref[...], w_ref[...])
    halo_ref[j] = acc[tm - HALO:tm, :]
    conv = cb_ref[...] + cw_ref[CONV_W - 1:CONV_W, :] * acc
    for tap in range(CONV_W - 1):
        shifted = _shift_rows_down(acc, gp_ref[...], CONV_W - 1 - tap)
        conv = conv + cw_ref[tap:tap + 1, :] * shifted
    o_ref[...] = _silu(conv).astype(BF16)


def mixer_in_proj(x, gain, w, w_small, conv_w, conv_b, *, bsz, seq, tm, tn):
    t, d = x.shape
    n = w.shape[1]
    conv_dim = conv_w.shape[1]
    ns = w_small.shape[1]
    assert seq % tm == 0 and n % tn == 0
    ni, nj = seq // tm, n // tn
    identity = jnp.zeros((CONV_W, n - conv_dim), F32).at[CONV_W - 1].set(1.0)
    taps = jnp.concatenate([conv_w.astype(F32), identity], axis=1)
    bias = jnp.pad(conv_b.astype(F32), (0, n - conv_dim)).reshape(1, n)
    return pl.pallas_call(
        _inproj_kernel,
        grid=(bsz, ni, nj),
        in_specs=[
            pl.BlockSpec((tm, d), lambda b, i, j: (b * ni + i, 0)),
            pl.BlockSpec((1, d), lambda b, i, j: (0, 0)),
            pl.BlockSpec((d, tn), lambda b, i, j: (0, j)),
            pl.BlockSpec((CONV_W, tn), lambda b, i, j: (0, j)),
            pl.BlockSpec((1, tn), lambda b, i, j: (0, j)),
            pl.BlockSpec((d, ns), lambda b, i, j: (0, 0)),
        ],
        out_specs=[
            pl.BlockSpec((tm, tn), lambda b, i, j: (b * ni + i, j)),
            pl.BlockSpec((tm, ns), lambda b, i, j: (b * ni + i, 0)),
        ],
        out_shape=[jax.ShapeDtypeStruct((t, n), BF16), jax.ShapeDtypeStruct((t, ns), F32)],
        scratch_shapes=[
            pltpu.VMEM((tm, d), BF16),
            pltpu.VMEM((HALO, tn), F32),
            pltpu.VMEM((nj, HALO, tn), F32),
        ],
        compiler_params=_cparams(("parallel", "arbitrary", "arbitrary")),
        name="mixer_in_proj",
    )(x, gain.reshape(1, d), w, taps, bias, w_small)


def _split2(x):
    hi = x.astype(BF16)
    return hi, (x - hi.astype(F32)).astype(BF16)


def _select(parts, sel_stacked):
    k = parts[0].shape[1]
    return _dot(jnp.concatenate(parts, axis=1), sel_stacked[0:len(parts) * k, :])


def _ssd_kernel(xbc_ref, zs_ref, dt_ref, dtb_ref, alog_ref, dskip_ref, nw_ref,
                ehead_ref, o_ref, s_ref, *, q, groups, inner, state):
    c = pl.program_id(1)
    gw = inner // groups
    hpg = gw // M_HEAD_DIM

    @pl.when(c == 0)
    def _():
        s_ref[...] = jnp.zeros_like(s_ref)

    dt = _softplus(dt_ref[...] + dtb_ref[...])
    a_neg = -jnp.exp(alog_ref[...])
    tril = _tril_mask(q)
    acum = _dot_sel_lhs(_as_sel(tril), dt * a_neg)
    acum_t = acum.T
    dt3 = _split3(dt)
    ac3 = _split3(acum)
    lane_head = lax.broadcasted_iota(jnp.int32, (q, gw), 1) // M_HEAD_DIM
    head_sel = [jnp.where(lane_head == hh, 1.0, 0.0).astype(BF16) for hh in range(hpg)]

    for g in range(groups):
        xs = xbc_ref[:, g * gw:(g + 1) * gw].astype(F32)
        bm16 = xbc_ref[:, inner + g * state:inner + (g + 1) * state]
        c_lo = inner + groups * state + g * state
        cm16 = xbc_ref[:, c_lo:c_lo + state]
        e_g = ehead_ref[:, g * gw:(g + 1) * gw]
        dt_x = _select(dt3, e_g)
        ac_x = _select(ac3, e_g)
        xdt = xs * dt_x
        xdt16 = xdt.astype(BF16)
        cb = _dot_nt(cm16, bm16)
        lhs, rhs = [], []
        for hh in range(hpg):
            h = g * hpg + hh
            diff = jnp.broadcast_to(acum[:, h:h + 1], (q, q)) - acum_t[h:h + 1, :]
            dec = jnp.exp(jnp.where(tril, diff, NEG_INF))
            lhs.append((cb * dec).astype(BF16))
            rhs.append(xdt16 * head_sel[hh])
        y = _dot(jnp.concatenate(lhs, axis=1), jnp.concatenate(rhs, axis=0))
        s_old = s_ref[g]
        y = y + _dot(cm16, s_old.astype(BF16)) * jnp.exp(ac_x)
        y = y + dskip_ref[:, g * gw:(g + 1) * gw] * xs
        last = ac_x[q - 1:q, :]
        xw = (xdt * jnp.exp(last - ac_x)).astype(BF16)
        s_ref[g] = s_old * jnp.exp(last) + _dot_tn(bm16, xw)
        y = y * zs_ref[:, g * gw:(g + 1) * gw].astype(F32)
        o_ref[:, g * gw:(g + 1) * gw] = (y * _rms_scale(y) * nw_ref[:, g * gw:(g + 1) * gw]).astype(BF16)


def ssd_mixer_core(act, dt_raw, dt_bias, a_log, d_skip, norm_w, *, bsz, seq, inner, heads):
    t = act.shape[0]
    q = SSD_CHUNK
    nc = seq // q
    conv_dim = inner + 2 * M_GROUPS * M_STATE
    assert conv_dim == 2 * inner and heads <= LANES_V7X
    pad = LANES_V7X - heads
    head_ids = jnp.arange(LANES_V7X)
    n_split = 3
    ehead = (head_ids[:, None] == (jnp.arange(inner) // M_HEAD_DIM)[None, :]).astype(BF16)
    ehead = jnp.concatenate([ehead] * n_split, axis=0)
    z_blk = conv_dim // inner
    row = lambda b, c: b * nc + c
    const = lambda shape: pl.BlockSpec(shape, lambda b, c: (0, 0))
    return pl.pallas_call(
        functools.partial(_ssd_kernel, q=q, groups=M_GROUPS, inner=inner, state=M_STATE),
        grid=(bsz, nc),
        in_specs=[
            pl.BlockSpec((q, conv_dim), lambda b, c: (row(b, c), 0)),
            pl.BlockSpec((q, inner), lambda b, c: (row(b, c), z_blk)),
            pl.BlockSpec((q, LANES_V7X), lambda b, c: (row(b, c), 0)),
            const((1, LANES_V7X)),
            const((1, LANES_V7X)),
            const((1, inner)),
            const((1, inner)),
            const((n_split * LANES_V7X, inner)),
        ],
        out_specs=pl.BlockSpec((q, inner), lambda b, c: (row(b, c), 0)),
        out_shape=jax.ShapeDtypeStruct((t, inner), BF16),
        scratch_shapes=[pltpu.VMEM((M_GROUPS, M_STATE, inner // M_GROUPS), F32)],
        compiler_params=_cparams(("parallel", "arbitrary")),
        name="ssd_core",
    )(act, act, dt_raw,
      jnp.pad(dt_bias, (0, pad)).reshape(1, LANES_V7X), jnp.pad(a_log, (0, pad)).reshape(1, LANES_V7X),
      jnp.repeat(d_skip, M_HEAD_DIM).reshape(1, inner), norm_w.reshape(1, inner), ehead)


def _hgrn2_kernel(q_ref, f_ref, i_ref, g_ref, lb_ref, nw_ref, o_ref, s_ref, *, q, heads):
    c = pl.program_id(1)
    dk = H_EXPAND

    @pl.when(c == 0)
    def _():
        s_ref[...] = jnp.zeros_like(s_ref)

    lb = lb_ref[...]
    forget = lb + (1.0 - lb) * jax.nn.sigmoid(f_ref[...])
    tril = _tril_mask(q)
    gc = _dot_sel_lhs(_as_sel(tril), jnp.log(forget))
    key = 1.0 - forget
    qs = _silu(q_ref[...]) * (dk ** -0.5)
    mid = q // 2 - 1
    g_mid = gc[mid:mid + 1, :]
    g_last = gc[q - 1:q, :]
    q_dec = (qs * jnp.exp(gc - g_mid)).astype(BF16)
    k_inv = (key * jnp.exp(g_mid - gc)).astype(BF16)
    q_in = (qs * jnp.exp(gc)).astype(BF16)
    k_end = (key * jnp.exp(g_last - gc)).astype(BF16)
    e_last = jnp.exp(g_last)
    for h in range(heads):
        sl = slice(h * dk, (h + 1) * dk)
        v_h = i_ref[:, sl].astype(BF16)
        att = jnp.where(tril, _dot_nt(q_dec[:, sl], k_inv[:, sl]), 0.0)
        st_old = s_ref[h]
        o_h = _dot(att.astype(BF16), v_h) + _dot_nt(q_in[:, sl], st_old.astype(BF16))
        s_ref[h] = st_old * e_last[:, sl] + _dot_tn(v_h, k_end[:, sl])
        o_h = o_h * _rms_scale(o_h) * nw_ref[...]
        o_ref[:, sl] = (o_h * _silu(g_ref[:, sl])).astype(BF16)


def hgrn2_mixer_core(proj, lower_bound, norm_w, *, bsz, seq, d):
    t = proj.shape[0]
    q = HGRN_CHUNK
    nc = seq // q
    heads = d // H_EXPAND
    row = lambda b, c: b * nc + c
    part = lambda k: pl.BlockSpec((q, d), lambda b, c: (row(b, c), k))
    return pl.pallas_call(
        functools.partial(_hgrn2_kernel, q=q, heads=heads),
        grid=(bsz, nc),
        in_specs=[part(0), part(1), part(2), part(3),
                  pl.BlockSpec((1, d), lambda b, c: (0, 0)),
                  pl.BlockSpec((1, H_EXPAND), lambda b, c: (0, 0))],
        out_specs=pl.BlockSpec((q, d), lambda b, c: (row(b, c), 0)),
        out_shape=jax.ShapeDtypeStruct((t, d), BF16),
        scratch_shapes=[pltpu.VMEM((heads, H_EXPAND, d // heads), F32)],
        compiler_params=_cparams(("parallel", "arbitrary")),
        name="hgrn2_core",
    )(proj, proj, proj, proj, lower_bound.reshape(1, d), norm_w.reshape(1, H_EXPAND))


def _block_diag(blocks):
    n = len(blocks)
    rows = []
    for e, blk in enumerate(blocks):
        zero = jnp.zeros_like(blk)
        rows.append(jnp.concatenate([blk if k == e else zero for k in range(n)], axis=1))
    return jnp.concatenate(rows, axis=0)


def _gdn_kernel(qkv_ref, zs_ref, ba_ref, dtb_ref, alog_ref, nw_ref,
                o_ref, s_ref, *, q, qk_heads, v_heads):
    c = pl.program_id(1)
    dh = G_HEAD_DIM
    key_dim = qk_heads * dh
    rep = v_heads // qk_heads
    assert rep == 2 and GDN_QUAD % rep == 0 and 2 * q == LANES_V7X

    @pl.when(c == 0)
    def _():
        s_ref[...] = jnp.zeros_like(s_ref)

    tril = _tril_mask(q)
    beta = jax.nn.sigmoid(ba_ref[:, 0:LANES_V7X])
    gate = -jnp.exp(alog_ref[...]) * _softplus(ba_ref[:, LANES_V7X:2 * LANES_V7X] + dtb_ref[...])
    gc = _dot_sel_lhs(_as_sel(tril), gate)

    eye_t = (lax.broadcasted_iota(jnp.int32, (q, v_heads * q), 0)
             == lax.broadcasted_iota(jnp.int32, (q, v_heads * q), 1) % q)
    col_s = lax.broadcasted_iota(jnp.int32, (q, v_heads * q), 1) % q
    row_l = lax.broadcasted_iota(jnp.int32, (q, v_heads * q), 0)
    lane_lo = lax.broadcasted_iota(jnp.int32, (q, 2 * q), 1) < q
    g_wide = [jnp.broadcast_to(gc[:, h:h + 1], (q, dh)) for h in range(v_heads)]
    b_wide = [jnp.broadcast_to(beta[:, h:h + 1], (q, dh)) for h in range(v_heads)]
    g_col = jnp.concatenate([jnp.where(lane_lo, g_wide[h], g_wide[h + 1]) for h in range(0, v_heads, 2)], axis=1)
    b_col = jnp.concatenate([jnp.where(lane_lo, b_wide[h], b_wide[h + 1]) for h in range(0, v_heads, 2)], axis=1)
    g_row = jnp.sum(jnp.where(eye_t, g_col, 0.0), axis=0, keepdims=True)
    decay = jnp.exp(jnp.where(row_l >= col_s, g_col - g_row, NEG_INF))

    kk_parts, qk_parts, k_heads, k_f32, q_heads = [], [], [], [], []
    for p in range(qk_heads):
        q_p = qkv_ref[:, p * dh:(p + 1) * dh].astype(F32)
        k_p = qkv_ref[:, key_dim + p * dh:key_dim + (p + 1) * dh].astype(F32)
        q_p = q_p * lax.rsqrt(jnp.sum(q_p * q_p, axis=-1, keepdims=True) + EPS) * (dh ** -0.5)
        k_p = k_p * lax.rsqrt(jnp.sum(k_p * k_p, axis=-1, keepdims=True) + EPS)
        k16 = k_p.astype(BF16)
        q16 = q_p.astype(BF16)
        k_rep = jnp.concatenate([k16] * rep, axis=0)
        kk_parts.append(_dot_nt(k16, k_rep))
        qk_parts.append(_dot_nt(q16, k_rep))
        k_heads.append(k16)
        k_f32.append(k_p)
        q_heads.append(q16)
    kk = jnp.concatenate(kk_parts, axis=1)
    qk = jnp.concatenate(qk_parts, axis=1)
    m_all = jnp.where(row_l > col_s, b_col * kk * decay, 0.0)
    att_all = qk * decay

    quad_w = GDN_QUAD * q
    n_quads = v_heads // GDN_QUAD
    lane_blk = lax.broadcasted_iota(jnp.int32, (q, quad_w), 1) // q
    blk_sel = [jnp.where(lane_blk == e, 1.0, 0.0).astype(BF16) for e in range(GDN_QUAD)]

    def blocks_on_diagonal(x16):
        return jnp.concatenate([x16 * blk_sel[e] for e in range(GDN_QUAD)], axis=0)

    def times_blockwise(lhs_hi, lhs_lo, bd_hi, bd_lo):
        lhs = jnp.concatenate([jnp.concatenate([hi, lo, hi], axis=1) for hi, lo in zip(lhs_hi, lhs_lo)], axis=0)
        out = _dot(lhs, jnp.concatenate([bd_hi, bd_hi, bd_lo], axis=0))
        return [out[i * q:(i + 1) * q, :] for i in range(len(lhs_hi))]

    levels = q.bit_length() - 2
    p_cur = [-m_all[:, u * quad_w:(u + 1) * quad_w] for u in range(n_quads)]
    n_acc = list(p_cur)
    for lvl in range(levels + 1):
        for u in range(n_quads):
            p_hi, p_lo = _split2(p_cur[u])
            bd_hi, bd_lo = blocks_on_diagonal(p_hi), blocks_on_diagonal(p_lo)
            if lvl == 0:
                (p_cur[u],) = times_blockwise([p_hi], [p_lo], bd_hi, bd_lo)
            elif lvl < levels:
                n_hi, n_lo = _split2(n_acc[u])
                n_p, p_sq = times_blockwise([n_hi, p_hi], [n_lo, p_lo], bd_hi, bd_lo)
                n_acc[u] = n_acc[u] + p_cur[u] + n_p
                p_cur[u] = p_sq
            else:
                n_hi, n_lo = _split2(n_acc[u])
                (n_p,) = times_blockwise([n_hi], [n_lo], bd_hi, bd_lo)
                n_acc[u] = n_acc[u] + p_cur[u] + n_p

    def two_pass(a, rhs16):
        a_hi, a_lo = _split2(a)
        return _dot(jnp.concatenate([a_hi, a_lo], axis=1), jnp.concatenate([rhs16, rhs16], axis=0))

    b_row = jnp.sum(jnp.where(eye_t, b_col, 0.0), axis=0, keepdims=True)
    bg_row = b_row * jnp.exp(g_row)
    eye_f = jnp.where(eye_t, 1.0, 0.0)
    zero_blk = jnp.zeros((dh, dh), BF16)
    pairs_per_quad = GDN_QUAD // rep
    for u in range(n_quads):
        cols = slice(u * quad_w, (u + 1) * quad_w)
        t_b = (n_acc[u] + eye_f[:, cols]) * b_row[:, cols]
        t_bg = (n_acc[u] + eye_f[:, cols]) * bg_row[:, cols]
        v16 = [qkv_ref[:, 2 * key_dim + h * dh:2 * key_dim + (h + 1) * dh]
               for h in range(u * GDN_QUAD, (u + 1) * GDN_QUAD)]
        k16 = [k_heads[h // rep] for h in range(u * GDN_QUAD, (u + 1) * GDN_QUAD)]
        u_quad = two_pass(t_b, _block_diag(v16))
        w_quad = two_pass(t_bg, _block_diag(k16))
        v_new, o_inter = [], []
        for pi in range(pairs_per_quad):
            p = u * pairs_per_quad + pi
            pc = slice(pi * rep * dh, (pi + 1) * rep * dh)
            g_pair = jnp.concatenate(g_wide[p * rep:(p + 1) * rep], axis=1)
            g_last = g_pair[q - 1:q, :]
            s_old = s_ref[p]
            s16 = s_old.astype(BF16)
            s_bd = jnp.concatenate([jnp.concatenate([s16[:, 0:dh], zero_blk], axis=1),
                                    jnp.concatenate([zero_blk, s16[:, dh:2 * dh]], axis=1)], axis=0)
            lhs = jnp.concatenate([w_quad[:, pc].astype(BF16), jnp.concatenate([q_heads[p]] * rep, axis=1)],
                                  axis=0)
            both = _dot(lhs, s_bd)
            vn = u_quad[:, pc] - both[0:q, :]
            o_inter.append(both[q:2 * q, :] * jnp.exp(g_pair))
            vn16 = vn.astype(BF16)
            zero_q = jnp.zeros((q, dh), BF16)
            vn_bd = jnp.concatenate([jnp.concatenate([vn16[:, 0:dh], zero_q], axis=1),
                                     jnp.concatenate([zero_q, vn16[:, dh:2 * dh]], axis=1)], axis=0)
            k_dec = jnp.concatenate([k_f32[p]] * rep, axis=1) * jnp.exp(g_last - g_pair)
            ke = jnp.concatenate([k_dec[:, 0:dh], k_dec[:, dh:2 * dh]], axis=0).astype(BF16)
            s_ref[p] = s_old * jnp.exp(g_last) + _dot_tn(ke, vn_bd)
            v_new.extend([vn16[:, 0:dh], vn16[:, dh:2 * dh]])
        o_intra = _dot(att_all[:, cols].astype(BF16), _block_diag(v_new))
        o_quad = jnp.concatenate(o_inter, axis=1) + o_intra
        for e in range(GDN_QUAD):
            h = u * GDN_QUAD + e
            sl = slice(h * dh, (h + 1) * dh)
            o_h = o_quad[:, e * dh:(e + 1) * dh]
            o_h = o_h * _rms_scale(o_h) * nw_ref[...]
            o_ref[:, sl] = (o_h * zs_ref[:, sl].astype(F32)).astype(BF16)


def gdn_mixer_core(act, ba_raw, a_log, dt_bias, norm_w, *, bsz, seq, qk_heads, v_heads):
    t = act.shape[0]
    q = GDN_CHUNK
    nc = seq // q
    dh = G_HEAD_DIM
    key_dim = qk_heads * dh
    val_dim = v_heads * dh
    conv_dim = 2 * key_dim + val_dim
    assert conv_dim == 2 * val_dim and v_heads % GDN_QUAD == 0
    pad = LANES_V7X - v_heads
    z_blk = conv_dim // val_dim
    row = lambda b, c: b * nc + c
    const = lambda shape: pl.BlockSpec(shape, lambda b, c: (0, 0))
    return pl.pallas_call(
        functools.partial(_gdn_kernel, q=q, qk_heads=qk_heads, v_heads=v_heads),
        grid=(bsz, nc),
        in_specs=[
            pl.BlockSpec((q, conv_dim), lambda b, c: (row(b, c), 0)),
            pl.BlockSpec((q, val_dim), lambda b, c: (row(b, c), z_blk)),
            pl.BlockSpec((q, 2 * LANES_V7X), lambda b, c: (row(b, c), 0)),
            const((1, LANES_V7X)),
            const((1, LANES_V7X)),
            const((1, dh)),
        ],
        out_specs=pl.BlockSpec((q, val_dim), lambda b, c: (row(b, c), 0)),
        out_shape=jax.ShapeDtypeStruct((t, val_dim), BF16),
        scratch_shapes=[pltpu.VMEM((qk_heads, dh, (v_heads // qk_heads) * dh), F32)],
        compiler_params=_cparams(("parallel", "arbitrary")),
        name="gdn_core",
    )(act, act, ba_raw, jnp.pad(dt_bias, (0, pad)).reshape(1, LANES_V7X),
      jnp.pad(a_log, (0, pad)).reshape(1, LANES_V7X), norm_w.reshape(1, dh))


def _pad_cols(w, width):
    return jnp.pad(w, ((0, 0), (0, width - w.shape[1])))


def _ssd_in_weights(in_w, inner, conv_dim):
    z = in_w[:, :inner]
    xbc = in_w[:, inner:inner + conv_dim]
    dt = _pad_cols(in_w[:, inner + conv_dim:], LANES_V7X)
    return jnp.concatenate([xbc, z], axis=1).astype(BF16), dt.astype(BF16)


def _gdn_in_weights(in_w, conv_dim, val_dim, v_heads):
    main = in_w[:, :conv_dim + val_dim]
    b = _pad_cols(in_w[:, conv_dim + val_dim:conv_dim + val_dim + v_heads], LANES_V7X)
    a = _pad_cols(in_w[:, conv_dim + val_dim + v_heads:], LANES_V7X)
    return main.astype(BF16), jnp.concatenate([b, a], axis=1).astype(BF16)


def _pick_tile(n, candidates):
    for cand in candidates:
        if n % cand == 0:
            return cand
    raise ValueError(f"no tile for {n} in {candidates}")


def kernel(x, mem, ln_mix, ln_xattn, ln_mem, ln_ffn, final_norm, m_in_w, m_conv_w, m_conv_b, m_dt_bias, m_a_log, m_d, m_norm_w, m_out_w, h_in_w, h_lower_bounds, h_norm_w, h_out_w, g_in_w, g_conv_w, g_a_log, g_dt_bias, g_norm_w, g_out_w, xa_q, xa_kv, xa_o, f_up, f_conv_w, f_conv_b, f_down):
    bsz, seq, d = x.shape
    n_mem = mem.shape[1]
    depth = ln_mix.shape[0]
    t = bsz * seq
    tm = _pick_tile(t, (1024, 512, 256, 128, 64))
    tq = _pick_tile(seq, (512, 256, 128, 64))
    ts = _pick_tile(seq, (1024, 512, 256, 128, 64))

    lb = jnp.cumsum(jax.nn.softmax(h_lower_bounds.astype(F32), axis=0), axis=0)
    lb = lb - lb[:1]

    m_inner = m_out_w.shape[1]
    m_heads = m_dt_bias.shape[1]
    m_conv_dim = m_conv_w.shape[2]
    g_v_heads = g_a_log.shape[1]
    g_val_dim = g_out_w.shape[1]
    g_conv_dim = g_conv_w.shape[2]
    g_qk_heads = (g_conv_dim - g_val_dim) // (2 * G_HEAD_DIM)
    d_ff = f_down.shape[1]

    xf = x.reshape(t, d)
    memf = mem.reshape(bsz * n_mem, d)
    ia = ib = ic = 0
    for i in range(depth):
        kind = i % N_MIXERS
        if kind == 0:
            w_in, w_dt = _ssd_in_weights(m_in_w[ia], m_inner, m_conv_dim)
            act, dt_raw = mixer_in_proj(xf, ln_mix[i], w_in, w_dt, m_conv_w[ia], m_conv_b[ia],
                                        bsz=bsz, seq=seq, tm=ts, tn=1024)
            y = ssd_mixer_core(act, dt_raw, m_dt_bias[ia], m_a_log[ia], m_d[ia],
                               m_norm_w[ia], bsz=bsz, seq=seq, inner=m_inner, heads=m_heads)
            xf = matmul_residual(y, m_out_w[ia].astype(BF16), xf, tm=tq)
            ia += 1
        elif kind == 1:
            proj = norm_matmul(xf, ln_mix[i], h_in_w[ib].astype(BF16), tm=tm, tn=1024)
            y = hgrn2_mixer_core(proj, lb[i], h_norm_w[ib], bsz=bsz, seq=seq, d=d)
            xf = matmul_residual(y, h_out_w[ib].astype(BF16), xf, tm=tq)
            ib += 1
        else:
            w_in, w_ba = _gdn_in_weights(g_in_w[ic], g_conv_dim, g_val_dim, g_v_heads)
            act, ba_raw = mixer_in_proj(xf, ln_mix[i], w_in, w_ba, g_conv_w[ic], jnp.zeros((g_conv_dim,), F32),
                                        bsz=bsz, seq=seq, tm=ts, tn=1024)
            y = gdn_mixer_core(act, ba_raw, g_a_log[ic], g_dt_bias[ic], g_norm_w[ic],
                               bsz=bsz, seq=seq, qk_heads=g_qk_heads, v_heads=g_v_heads)
            xf = matmul_residual(y, g_out_w[ic].astype(BF16), xf, tm=tq)
            ic += 1

        kv = norm_matmul(memf, ln_mem[i], xa_kv[i].astype(BF16), tm=_pick_tile(bsz * n_mem, (1024, 512, 256)),
                         tn=1024, out_dtype=BF16)
        xf = memory_cross_attention(xf, ln_xattn[i], xa_q[i].astype(BF16), kv, xa_o[i].astype(BF16),
                                    bsz=bsz, seq=seq, n_mem=n_mem, tq=tq)

        up_w = f_up[i].astype(BF16)
        last = i == depth - 1
        xf = conv_glu_ffn(xf, ln_ffn[i], up_w[:, :d_ff], up_w[:, d_ff:], f_conv_w[i], f_conv_b[i],
                          f_down[i].astype(BF16), bsz=bsz, seq=seq, tm=tq,
                          final_gain=final_norm if last else None)
    return xf.reshape(bsz, seq, d)
```

```python
import functools

import jax
import jax.numpy as jnp
from jax import lax
from jax.experimental import pallas as pl
from jax.experimental.pallas import tpu as pltpu

F32 = jnp.float32
BF16 = jnp.bfloat16
EPS = 1e-6
NEG_INF = float("-inf")

LANES_V7X = 128
SUBLANES_V7X = 8
VMEM_LIMIT_BYTES_V7X = 56 * 1024 * 1024

CONV_W = 4
FFN_CONV_W = 3
HALO = SUBLANES_V7X

M_HEAD_DIM = 64
M_GROUPS = 8
M_STATE = 128
SSD_CHUNK = 128
SSD_CHUNKS_PER_STEP = 4
H_EXPAND = 128
HGRN_CHUNK = 64
HGRN_CHUNKS_PER_STEP = 8
G_HEAD_DIM = 128
GDN_CHUNK = 64
GDN_CHUNKS_PER_STEP = 4
GDN_QUAD = 4
X_HEADS = 4
N_MIXERS = 3


def _cparams(sem):
    return pltpu.CompilerParams(dimension_semantics=sem, vmem_limit_bytes=VMEM_LIMIT_BYTES_V7X)


def _dot(a, b):
    return jnp.dot(a, b, preferred_element_type=F32)


def _dot_nt(a, b):
    return lax.dot_general(a, b, (((1,), (1,)), ((), ())), preferred_element_type=F32)


def _dot_tn(a, b):
    return lax.dot_general(a, b, (((0,), (0,)), ((), ())), preferred_element_type=F32)


def _split3(x):
    hi = x.astype(BF16)
    r1 = x - hi.astype(F32)
    mid = r1.astype(BF16)
    lo = (r1 - mid.astype(F32)).astype(BF16)
    return hi, mid, lo


def _dot_sel_lhs(sel, x):
    parts = _split3(x)
    return _dot(jnp.concatenate([sel] * len(parts), axis=1), jnp.concatenate(parts, axis=0))


def _silu(x):
    return x * jax.nn.sigmoid(x)


def _softplus(x):
    return jnp.maximum(x, 0.0) + jnp.log1p(jnp.exp(-jnp.abs(x)))


def _rms_scale(x):
    return lax.rsqrt(jnp.mean(x * x, axis=-1, keepdims=True) + EPS)


def _tril_mask(n):
    row = lax.broadcasted_iota(jnp.int32, (n, n), 0)
    col = lax.broadcasted_iota(jnp.int32, (n, n), 1)
    return row >= col


def _as_sel(mask):
    return jnp.where(mask, 1.0, 0.0).astype(BF16)


def _shift_rows_down(x, history, s):
    rows, n = x.shape
    groups = x.reshape(rows // SUBLANES_V7X, SUBLANES_V7X, n)
    rolled = pltpu.roll(groups, s, axis=1)
    hist_rolled = pltpu.roll(history, s, axis=0).reshape(1, SUBLANES_V7X, n)
    prev = jnp.concatenate([hist_rolled, rolled[:-1]], axis=0)
    sub = lax.broadcasted_iota(jnp.int32, rolled.shape, 1)
    return jnp.where(sub < s, prev, rolled).reshape(rows, n)


def _norm_matmul_kernel(x_ref, g_ref, w_ref, o_ref, xn_ref):
    @pl.when(pl.program_id(1) == 0)
    def _():
        x = x_ref[...]
        xn_ref[...] = (x * _rms_scale(x) * g_ref[...]).astype(BF16)

    o_ref[...] = _dot(xn_ref[...], w_ref[...]).astype(o_ref.dtype)


def norm_matmul(x, gain, w, *, tm, tn, out_dtype=F32):
    t, d = x.shape
    n = w.shape[1]
    assert t % tm == 0 and n % tn == 0, (t, tm, n, tn)
    return pl.pallas_call(
        _norm_matmul_kernel,
        grid=(t // tm, n // tn),
        in_specs=[
            pl.BlockSpec((tm, d), lambda i, j: (i, 0)),
            pl.BlockSpec((1, d), lambda i, j: (0, 0)),
            pl.BlockSpec((d, tn), lambda i, j: (0, j)),
        ],
        out_specs=pl.BlockSpec((tm, tn), lambda i, j: (i, j)),
        out_shape=jax.ShapeDtypeStruct((t, n), out_dtype),
        scratch_shapes=[pltpu.VMEM((tm, d), BF16)],
        compiler_params=_cparams(("parallel", "arbitrary")),
        name="norm_matmul",
    )(x, gain.reshape(1, d), w)


def _matmul_res_kernel(a_ref, w_ref, r_ref, o_ref):
    o_ref[...] = r_ref[...] + _dot(a_ref[...], w_ref[...])


def matmul_residual(a, w, res, *, tm):
    t, k = a.shape
    d = w.shape[1]
    assert t % tm == 0
    return pl.pallas_call(
        _matmul_res_kernel,
        grid=(t // tm,),
        in_specs=[
            pl.BlockSpec((tm, k), lambda i: (i, 0)),
            pl.BlockSpec((k, d), lambda i: (0, 0)),
            pl.BlockSpec((tm, d), lambda i: (i, 0)),
        ],
        out_specs=pl.BlockSpec((tm, d), lambda i: (i, 0)),
        out_shape=jax.ShapeDtypeStruct((t, d), F32),
        compiler_params=_cparams(("parallel",)),
        name="matmul_residual",
    )(a, w, res)


def _xattn_kernel(x_ref, g_ref, wq_ref, kv_ref, wo_ref, o_ref, *, heads):
    x = x_ref[...]
    d = x.shape[1]
    dh = d // heads
    xn = (x * _rms_scale(x) * g_ref[...]).astype(BF16)
    q = (_dot(xn, wq_ref[...]) * (dh ** -0.5)).astype(BF16)
    outs = []
    for h in range(heads):
        k_h = kv_ref[:, h * dh:(h + 1) * dh]
        v_h = kv_ref[:, d + h * dh:d + (h + 1) * dh]
        s = _dot_nt(q[:, h * dh:(h + 1) * dh], k_h)
        p = jnp.exp(s - jnp.max(s, axis=-1, keepdims=True))
        o_h = _dot(p.astype(BF16), v_h) / jnp.sum(p, axis=-1, keepdims=True)
        outs.append(o_h.astype(BF16))
    o_ref[...] = x + _dot(jnp.concatenate(outs, axis=1), wo_ref[...])


def memory_cross_attention(x, gain, wq, kv, wo, *, bsz, seq, n_mem, tq):
    t, d = x.shape
    nq = seq // tq
    return pl.pallas_call(
        functools.partial(_xattn_kernel, heads=X_HEADS),
        grid=(bsz, nq),
        in_specs=[
            pl.BlockSpec((tq, d), lambda b, i: (b * nq + i, 0)),
            pl.BlockSpec((1, d), lambda b, i: (0, 0)),
            pl.BlockSpec((d, d), lambda b, i: (0, 0)),
            pl.BlockSpec((n_mem, 2 * d), lambda b, i: (b, 0)),
            pl.BlockSpec((d, d), lambda b, i: (0, 0)),
        ],
        out_specs=pl.BlockSpec((tq, d), lambda b, i: (b * nq + i, 0)),
        out_shape=jax.ShapeDtypeStruct((t, d), F32),
        compiler_params=_cparams(("parallel", "parallel")),
        name="memory_xattn",
    )(x, gain.reshape(1, d), wq, kv, wo)


def _ffn_kernel(x_ref, g_ref, wg_ref, wu_ref, cw_ref, cb_ref, wd_ref, *rest, final_norm):
    if final_norm:
        fg_ref, o_ref, act_ref, halo_ref = rest
    else:
        o_ref, act_ref, halo_ref = rest
    tm = x_ref.shape[0]
    f = wg_ref.shape[1]

    @pl.when(pl.program_id(1) == 0)
    def _():
        halo_ref[...] = jnp.zeros_like(halo_ref)

    x = x_ref[...]
    xn = (x * _rms_scale(x) * g_ref[...]).astype(BF16)
    cols_per = 2 * LANES_V7X
    for c in range(0, f, cols_per):
        cols = slice(c, c + cols_per)
        gate = _dot(xn, wg_ref[:, cols])
        up = _dot(xn, wu_ref[:, cols])
        history = halo_ref[:, cols]
        halo_ref[:, cols] = gate[tm - HALO:tm, :]
        conv = cb_ref[:, cols] + cw_ref[FFN_CONV_W - 1:FFN_CONV_W, cols] * gate
        for tap in range(FFN_CONV_W - 1):
            conv = conv + cw_ref[tap:tap + 1, cols] * _shift_rows_down(gate, history, FFN_CONV_W - 1 - tap)
        act_ref[:, cols] = (_silu(conv) * up).astype(BF16)
    y = x + _dot(act_ref[...], wd_ref[...])
    if final_norm:
        y = y * _rms_scale(y) * fg_ref[...]
    o_ref[...] = y


def conv_glu_ffn(x, gain, wg, wu, conv_w, conv_b, wd, *, bsz, seq, tm, final_gain=None):
    t, d = x.shape
    f = wg.shape[1]
    assert seq % tm == 0 and f % (2 * LANES_V7X) == 0
    ni = seq // tm
    resident = lambda shape: pl.BlockSpec(shape, lambda b, i: (0, 0), pipeline_mode=pl.Buffered(1))
    in_specs = [
        pl.BlockSpec((tm, d), lambda b, i: (b * ni + i, 0)),
        resident((1, d)),
        resident((d, f)),
        resident((d, f)),
        resident((FFN_CONV_W, f)),
        resident((1, f)),
        resident((f, d)),
    ]
    args = [x, gain.reshape(1, d), wg, wu, conv_w, conv_b.reshape(1, f), wd]
    if final_gain is not None:
        in_specs.append(resident((1, d)))
        args.append(final_gain.reshape(1, d))
    return pl.pallas_call(
        functools.partial(_ffn_kernel, final_norm=final_gain is not None),
        grid=(bsz, ni),
        in_specs=in_specs,
        out_specs=pl.BlockSpec((tm, d), lambda b, i: (b * ni + i, 0)),
        out_shape=jax.ShapeDtypeStruct((t, d), F32),
        scratch_shapes=[pltpu.VMEM((tm, f), BF16), pltpu.VMEM((HALO, f), F32)],
        compiler_params=_cparams(("parallel", "arbitrary")),
        name="conv_glu_ffn",
    )(*args)


def _inproj_kernel(x_ref, g_ref, w_ref, cw_ref, cb_ref, ws_ref, o_ref, os_ref, xn_ref, gp_ref, halo_ref,
                   *, n_conv_tiles):
    i = pl.program_id(1)
    j = pl.program_id(2)
    tm = x_ref.shape[0]

    @pl.when(j == 0)
    def _():
        x = x_ref[...]
        xn = (x * _rms_scale(x) * g_ref[...]).astype(BF16)
        xn_ref[...] = xn
        os_ref[...] = _dot(xn, ws_ref[...])

    @pl.when(i == 0)
    def _():
        gp_ref[...] = jnp.zeros_like(gp_ref)

    @pl.when(jnp.logical_and(i > 0, j < n_conv_tiles))
    def _():
        gp_ref[...] = halo_ref[j]

    @pl.when(j < n_conv_tiles)
    def _():
        acc = _dot(xn_ref[...], w_ref[...])
        halo_ref[j] = acc[tm - HALO:tm, :]
        conv = cb_ref[...] + cw_ref[CONV_W - 1:CONV_W, :] * acc
        for tap in range(CONV_W - 1):
            conv = conv + cw_ref[tap:tap + 1, :] * _shift_rows_down(acc, gp_ref[...], CONV_W - 1 - tap)
        o_ref[...] = _silu(conv).astype(BF16)

    @pl.when(j >= n_conv_tiles)
    def _():
        o_ref[...] = _silu(_dot(xn_ref[...], w_ref[...])).astype(BF16)


def mixer_in_proj(x, gain, w, w_small, conv_w, conv_b, *, bsz, seq, tm, tn):
    t, d = x.shape
    n = w.shape[1]
    conv_dim = conv_w.shape[1]
    ns = w_small.shape[1]
    assert seq % tm == 0 and n % tn == 0 and conv_dim % tn == 0
    ni, nj, n_conv_tiles = seq // tm, n // tn, conv_dim // tn
    conv_col = lambda b, i, j: (0, jnp.minimum(j, n_conv_tiles - 1))
    return pl.pallas_call(
        functools.partial(_inproj_kernel, n_conv_tiles=n_conv_tiles),
        grid=(bsz, ni, nj),
        in_specs=[
            pl.BlockSpec((tm, d), lambda b, i, j: (b * ni + i, 0)),
            pl.BlockSpec((1, d), lambda b, i, j: (0, 0)),
            pl.BlockSpec((d, tn), lambda b, i, j: (0, j)),
            pl.BlockSpec((CONV_W, tn), conv_col),
            pl.BlockSpec((1, tn), conv_col),
            pl.BlockSpec((d, ns), lambda b, i, j: (0, 0)),
        ],
        out_specs=[
            pl.BlockSpec((tm, tn), lambda b, i, j: (b * ni + i, j)),
            pl.BlockSpec((tm, ns), lambda b, i, j: (b * ni + i, 0)),
        ],
        out_shape=[jax.ShapeDtypeStruct((t, n), BF16), jax.ShapeDtypeStruct((t, ns), F32)],
        scratch_shapes=[
            pltpu.VMEM((tm, d), BF16),
            pltpu.VMEM((HALO, tn), F32),
            pltpu.VMEM((n_conv_tiles, HALO, tn), F32),
        ],
        compiler_params=_cparams(("parallel", "arbitrary", "arbitrary")),
        name="mixer_in_proj",
    )(x, gain.reshape(1, d), w, conv_w, conv_b.reshape(1, conv_dim), w_small)


def _split2(x):
    hi = x.astype(BF16)
    return hi, (x - hi.astype(F32)).astype(BF16)


def _select(parts, sel_stacked):
    k = parts[0].shape[1]
    return _dot(jnp.concatenate(parts, axis=1), sel_stacked[0:len(parts) * k, :])


def _ssd_kernel(xbc_ref, zs_ref, dt_ref, dtb_ref, alog_ref, dskip_ref, nw_ref,
                ehead_ref, o_ref, s_ref, *, q, groups, inner, state):
    @pl.when(pl.program_id(1) == 0)
    def _():
        s_ref[...] = jnp.zeros_like(s_ref)

    for sc in range(xbc_ref.shape[0] // q):
        rows = pl.ds(sc * q, q)
        _ssd_chunk(xbc_ref.at[rows], zs_ref.at[rows], dt_ref.at[rows], dtb_ref, alog_ref, dskip_ref, nw_ref,
                   ehead_ref, o_ref.at[rows], s_ref, q=q, groups=groups, inner=inner, state=state)


def _ssd_chunk(xbc_ref, zs_ref, dt_ref, dtb_ref, alog_ref, dskip_ref, nw_ref,
               ehead_ref, o_ref, s_ref, *, q, groups, inner, state):
    gw = inner // groups
    hpg = gw // M_HEAD_DIM
    dt = _softplus(dt_ref[...] + dtb_ref[...])
    a_neg = -jnp.exp(alog_ref[...])
    tril = _tril_mask(q)
    acum = _dot_sel_lhs(_as_sel(tril), dt * a_neg)
    acum_t = acum.T
    dt3 = _split3(dt)
    ac3 = _split3(acum)
    lane_head = lax.broadcasted_iota(jnp.int32, (q, gw), 1) // M_HEAD_DIM
    head_sel = [jnp.where(lane_head == hh, 1.0, 0.0).astype(BF16) for hh in range(hpg)]

    for g in range(groups):
        xs = xbc_ref[:, g * gw:(g + 1) * gw].astype(F32)
        bm16 = xbc_ref[:, inner + g * state:inner + (g + 1) * state]
        c_lo = inner + groups * state + g * state
        cm16 = xbc_ref[:, c_lo:c_lo + state]
        e_g = ehead_ref[:, g * gw:(g + 1) * gw]
        dt_x = _select(dt3, e_g)
        ac_x = _select(ac3, e_g)
        xdt = xs * dt_x
        xdt16 = xdt.astype(BF16)
        cb = _dot_nt(cm16, bm16)
        lhs, rhs = [], []
        for hh in range(hpg):
            h = g * hpg + hh
            diff = jnp.broadcast_to(acum[:, h:h + 1], (q, q)) - acum_t[h:h + 1, :]
            dec = jnp.exp(jnp.where(tril, diff, NEG_INF))
            lhs.append((cb * dec).astype(BF16))
            rhs.append(xdt16 * head_sel[hh])
        y = _dot(jnp.concatenate(lhs, axis=1), jnp.concatenate(rhs, axis=0))
        s_old = s_ref[g]
        y = y + _dot(cm16, s_old.astype(BF16)) * jnp.exp(ac_x)
        y = y + dskip_ref[:, g * gw:(g + 1) * gw] * xs
        last = ac_x[q - 1:q, :]
        xw = (xdt * jnp.exp(last - ac_x)).astype(BF16)
        s_ref[g] = s_old * jnp.exp(last) + _dot_tn(bm16, xw)
        y = y * zs_ref[:, g * gw:(g + 1) * gw].astype(F32)
        o_ref[:, g * gw:(g + 1) * gw] = (y * _rms_scale(y) * nw_ref[:, g * gw:(g + 1) * gw]).astype(BF16)


def ssd_mixer_core(act, dt_raw, dt_bias, a_log, d_skip, norm_w, *, bsz, seq, inner, heads):
    t = act.shape[0]
    q = SSD_CHUNK
    rows = q * SSD_CHUNKS_PER_STEP
    assert seq % rows == 0
    nc = seq // rows
    conv_dim = inner + 2 * M_GROUPS * M_STATE
    assert conv_dim == 2 * inner and heads <= LANES_V7X
    pad = LANES_V7X - heads
    head_ids = jnp.arange(LANES_V7X)
    n_split = 3
    ehead = (head_ids[:, None] == (jnp.arange(inner) // M_HEAD_DIM)[None, :]).astype(BF16)
    ehead = jnp.concatenate([ehead] * n_split, axis=0)
    z_blk = conv_dim // inner
    row = lambda b, c: b * nc + c
    const = lambda shape: pl.BlockSpec(shape, lambda b, c: (0, 0))
    return pl.pallas_call(
        functools.partial(_ssd_kernel, q=q, groups=M_GROUPS, inner=inner, state=M_STATE),
        grid=(bsz, nc),
        in_specs=[
            pl.BlockSpec((rows, conv_dim), lambda b, c: (row(b, c), 0)),
            pl.BlockSpec((rows, inner), lambda b, c: (row(b, c), z_blk)),
            pl.BlockSpec((rows, LANES_V7X), lambda b, c: (row(b, c), 0)),
            const((1, LANES_V7X)),
            const((1, LANES_V7X)),
            const((1, inner)),
            const((1, inner)),
            const((n_split * LANES_V7X, inner)),
        ],
        out_specs=pl.BlockSpec((rows, inner), lambda b, c: (row(b, c), 0)),
        out_shape=jax.ShapeDtypeStruct((t, inner), BF16),
        scratch_shapes=[pltpu.VMEM((M_GROUPS, M_STATE, inner // M_GROUPS), F32)],
        compiler_params=_cparams(("parallel", "arbitrary")),
        name="ssd_core",
    )(act, act, dt_raw,
      jnp.pad(dt_bias, (0, pad)).reshape(1, LANES_V7X), jnp.pad(a_log, (0, pad)).reshape(1, LANES_V7X),
      jnp.repeat(d_skip, M_HEAD_DIM).reshape(1, inner), norm_w.reshape(1, inner), ehead)


def _hgrn2_kernel(q_ref, f_ref, i_ref, g_ref, lb_ref, nw_ref, o_ref, s_ref, *, q, heads):
    c = pl.program_id(1)
    dk = H_EXPAND

    @pl.when(c == 0)
    def _():
        s_ref[...] = jnp.zeros_like(s_ref)

    lb = lb_ref[...]
    tril = _tril_mask(q)
    tril_sel = _as_sel(tril)
    for sc in range(q_ref.shape[0] // q):
        rows = slice(sc * q, (sc + 1) * q)
        forget = lb + (1.0 - lb) * jax.nn.sigmoid(f_ref[rows, :])
        gc = _dot_sel_lhs(tril_sel, jnp.log(forget))
        key = 1.0 - forget
        qs = _silu(q_ref[rows, :]) * (dk ** -0.5)
        mid = q // 2 - 1
        g_mid = gc[mid:mid + 1, :]
        g_last = gc[q - 1:q, :]
        q_dec = (qs * jnp.exp(gc - g_mid)).astype(BF16)
        k_inv = (key * jnp.exp(g_mid - gc)).astype(BF16)
        q_in = (qs * jnp.exp(gc)).astype(BF16)
        k_end = (key * jnp.exp(g_last - gc)).astype(BF16)
        e_last = jnp.exp(g_last)
        for h in range(heads):
            sl = slice(h * dk, (h + 1) * dk)
            v_h = i_ref[rows, sl].astype(BF16)
            att = jnp.where(tril, _dot_nt(q_dec[:, sl], k_inv[:, sl]), 0.0)
            st_old = s_ref[h]
            o_h = _dot(att.astype(BF16), v_h) + _dot_nt(q_in[:, sl], st_old.astype(BF16))
            s_ref[h] = st_old * e_last[:, sl] + _dot_tn(v_h, k_end[:, sl])
            o_h = o_h * _rms_scale(o_h) * nw_ref[...]
            o_ref[rows, sl] = (o_h * _silu(g_ref[rows, sl])).astype(BF16)


def hgrn2_mixer_core(proj, lower_bound, norm_w, *, bsz, seq, d):
    t = proj.shape[0]
    q = HGRN_CHUNK
    rows = q * HGRN_CHUNKS_PER_STEP
    assert seq % rows == 0
    nc = seq // rows
    heads = d // H_EXPAND
    row = lambda b, c: b * nc + c
    part = lambda k: pl.BlockSpec((rows, d), lambda b, c: (row(b, c), k))
    return pl.pallas_call(
        functools.partial(_hgrn2_kernel, q=q, heads=heads),
        grid=(bsz, nc),
        in_specs=[part(0), part(1), part(2), part(3),
                  pl.BlockSpec((1, d), lambda b, c: (0, 0)),
                  pl.BlockSpec((1, H_EXPAND), lambda b, c: (0, 0))],
        out_specs=pl.BlockSpec((rows, d), lambda b, c: (row(b, c), 0)),
        out_shape=jax.ShapeDtypeStruct((t, d), BF16),
        scratch_shapes=[pltpu.VMEM((heads, H_EXPAND, d // heads), F32)],
        compiler_params=_cparams(("parallel", "arbitrary")),
        name="hgrn2_core",
    )(proj, proj, proj, proj, lower_bound.reshape(1, d), norm_w.reshape(1, H_EXPAND))


def _block_diag(blocks):
    n = len(blocks)
    rows = []
    for e, blk in enumerate(blocks):
        zero = jnp.zeros_like(blk)
        rows.append(jnp.concatenate([blk if k == e else zero for k in range(n)], axis=1))
    return jnp.concatenate(rows, axis=0)


def _gdn_kernel(qkv_ref, zs_ref, ba_ref, dtb_ref, alog_ref, nw_ref, o_ref, s_ref, *, q, qk_heads, v_heads):
    @pl.when(pl.program_id(1) == 0)
    def _():
        s_ref[...] = jnp.zeros_like(s_ref)

    for sc in range(qkv_ref.shape[0] // q):
        rows = pl.ds(sc * q, q)
        _gdn_chunk(qkv_ref.at[rows], zs_ref.at[rows], ba_ref.at[rows], dtb_ref, alog_ref, nw_ref,
                   o_ref.at[rows], s_ref, q=q, qk_heads=qk_heads, v_heads=v_heads)


def _gdn_chunk(qkv_ref, zs_ref, ba_ref, dtb_ref, alog_ref, nw_ref, o_ref, s_ref, *, q, qk_heads, v_heads):
    dh = G_HEAD_DIM
    key_dim = qk_heads * dh
    rep = v_heads // qk_heads
    assert rep == 2 and GDN_QUAD % rep == 0 and 2 * q == LANES_V7X
    tril = _tril_mask(q)
    beta = jax.nn.sigmoid(ba_ref[:, 0:LANES_V7X])
    gate = -jnp.exp(alog_ref[...]) * _softplus(ba_ref[:, LANES_V7X:2 * LANES_V7X] + dtb_ref[...])
    gc = _dot_sel_lhs(_as_sel(tril), gate)

    eye_t = (lax.broadcasted_iota(jnp.int32, (q, v_heads * q), 0)
             == lax.broadcasted_iota(jnp.int32, (q, v_heads * q), 1) % q)
    col_s = lax.broadcasted_iota(jnp.int32, (q, v_heads * q), 1) % q
    row_l = lax.broadcasted_iota(jnp.int32, (q, v_heads * q), 0)
    lane_lo = lax.broadcasted_iota(jnp.int32, (q, 2 * q), 1) < q
    g_wide = [jnp.broadcast_to(gc[:, h:h + 1], (q, dh)) for h in range(v_heads)]
    b_wide = [jnp.broadcast_to(beta[:, h:h + 1], (q, dh)) for h in range(v_heads)]
    g_col = jnp.concatenate([jnp.where(lane_lo, g_wide[h], g_wide[h + 1]) for h in range(0, v_heads, 2)], axis=1)
    b_col = jnp.concatenate([jnp.where(lane_lo, b_wide[h], b_wide[h + 1]) for h in range(0, v_heads, 2)], axis=1)
    g_row = jnp.sum(jnp.where(eye_t, g_col, 0.0), axis=0, keepdims=True)
    decay = jnp.exp(jnp.where(row_l >= col_s, g_col - g_row, NEG_INF))

    kk_parts, qk_parts, k_heads, k_f32, q_heads = [], [], [], [], []
    for p in range(qk_heads):
        q_p = qkv_ref[:, p * dh:(p + 1) * dh].astype(F32)
        k_p = qkv_ref[:, key_dim + p * dh:key_dim + (p + 1) * dh].astype(F32)
        q_p = q_p * lax.rsqrt(jnp.sum(q_p * q_p, axis=-1, keepdims=True) + EPS) * (dh ** -0.5)
        k_p = k_p * lax.rsqrt(jnp.sum(k_p * k_p, axis=-1, keepdims=True) + EPS)
        k16 = k_p.astype(BF16)
        q16 = q_p.astype(BF16)
        k_rep = jnp.concatenate([k16] * rep, axis=0)
        kk_parts.append(_dot_nt(k16, k_rep))
        qk_parts.append(_dot_nt(q16, k_rep))
        k_heads.append(k16)
        k_f32.append(k_p)
        q_heads.append(q16)
    kk = jnp.concatenate(kk_parts, axis=1)
    qk = jnp.concatenate(qk_parts, axis=1)
    m_all = jnp.where(row_l > col_s, b_col * kk * decay, 0.0)
    att_all = qk * decay

    quad_w = GDN_QUAD * q
    n_quads = v_heads // GDN_QUAD
    lane_blk = lax.broadcasted_iota(jnp.int32, (q, quad_w), 1) // q
    blk_sel = [jnp.where(lane_blk == e, 1.0, 0.0).astype(BF16) for e in range(GDN_QUAD)]

    def blocks_on_diagonal(x16):
        return jnp.concatenate([x16 * blk_sel[e] for e in range(GDN_QUAD)], axis=0)

    def times_blockwise(lhs_hi, lhs_lo, bd_hi, bd_lo):
        lhs = jnp.concatenate([jnp.concatenate([hi, lo, hi], axis=1) for hi, lo in zip(lhs_hi, lhs_lo)], axis=0)
        out = _dot(lhs, jnp.concatenate([bd_hi, bd_hi, bd_lo], axis=0))
        return [out[i * q:(i + 1) * q, :] for i in range(len(lhs_hi))]

    levels = q.bit_length() - 2
    p_cur = [-m_all[:, u * quad_w:(u + 1) * quad_w] for u in range(n_quads)]
    n_acc = list(p_cur)
    for lvl in range(levels + 1):
        for u in range(n_quads):
            p_hi, p_lo = _split2(p_cur[u])
            bd_hi, bd_lo = blocks_on_diagonal(p_hi), blocks_on_diagonal(p_lo)
            if lvl == 0:
                (p_cur[u],) = times_blockwise([p_hi], [p_lo], bd_hi, bd_lo)
            elif lvl < levels:
                n_hi, n_lo = _split2(n_acc[u])
                n_p, p_sq = times_blockwise([n_hi, p_hi], [n_lo, p_lo], bd_hi, bd_lo)
                n_acc[u] = n_acc[u] + p_cur[u] + n_p
                p_cur[u] = p_sq
            else:
                n_hi, n_lo = _split2(n_acc[u])
                (n_p,) = times_blockwise([n_hi], [n_lo], bd_hi, bd_lo)
                n_acc[u] = n_acc[u] + p_cur[u] + n_p

    def two_pass(a, rhs16):
        a_hi, a_lo = _split2(a)
        return _dot(jnp.concatenate([a_hi, a_lo], axis=1), jnp.concatenate([rhs16, rhs16], axis=0))

    b_row = jnp.sum(jnp.where(eye_t, b_col, 0.0), axis=0, keepdims=True)
    bg_row = b_row * jnp.exp(g_row)
    eye_f = jnp.where(eye_t, 1.0, 0.0)
    zero_blk = jnp.zeros((dh, dh), BF16)
    pairs_per_quad = GDN_QUAD // rep
    for u in range(n_quads):
        cols = slice(u * quad_w, (u + 1) * quad_w)
        t_b = (n_acc[u] + eye_f[:, cols]) * b_row[:, cols]
        t_bg = (n_acc[u] + eye_f[:, cols]) * bg_row[:, cols]
        v16 = [qkv_ref[:, 2 * key_dim + h * dh:2 * key_dim + (h + 1) * dh]
               for h in range(u * GDN_QUAD, (u + 1) * GDN_QUAD)]
        k16 = [k_heads[h // rep] for h in range(u * GDN_QUAD, (u + 1) * GDN_QUAD)]
        u_quad = two_pass(t_b, _block_diag(v16))
        w_quad = two_pass(t_bg, _block_diag(k16))
        v_new, o_inter = [], []
        for pi in range(pairs_per_quad):
            p = u * pairs_per_quad + pi
            pc = slice(pi * rep * dh, (pi + 1) * rep * dh)
            g_pair = jnp.concatenate(g_wide[p * rep:(p + 1) * rep], axis=1)
            g_last = g_pair[q - 1:q, :]
            s_old = s_ref[p]
            s16 = s_old.astype(BF16)
            s_bd = jnp.concatenate([jnp.concatenate([s16[:, 0:dh], zero_blk], axis=1),
                                    jnp.concatenate([zero_blk, s16[:, dh:2 * dh]], axis=1)], axis=0)
            lhs = jnp.concatenate([w_quad[:, pc].astype(BF16), jnp.concatenate([q_heads[p]] * rep, axis=1)],
                                  axis=0)
            both = _dot(lhs, s_bd)
            vn = u_quad[:, pc] - both[0:q, :]
            o_inter.append(both[q:2 * q, :] * jnp.exp(g_pair))
            vn16 = vn.astype(BF16)
            zero_q = jnp.zeros((q, dh), BF16)
            vn_bd = jnp.concatenate([jnp.concatenate([vn16[:, 0:dh], zero_q], axis=1),
                                     jnp.concatenate([zero_q, vn16[:, dh:2 * dh]], axis=1)], axis=0)
            k_dec = jnp.concatenate([k_f32[p]] * rep, axis=1) * jnp.exp(g_last - g_pair)
            ke = jnp.concatenate([k_dec[:, 0:dh], k_dec[:, dh:2 * dh]], axis=0).astype(BF16)
            s_ref[p] = s_old * jnp.exp(g_last) + _dot_tn(ke, vn_bd)
            v_new.extend([vn16[:, 0:dh], vn16[:, dh:2 * dh]])
        o_intra = _dot(att_all[:, cols].astype(BF16), _block_diag(v_new))
        o_quad = jnp.concatenate(o_inter, axis=1) + o_intra
        for e in range(GDN_QUAD):
            h = u * GDN_QUAD + e
            sl = slice(h * dh, (h + 1) * dh)
            o_h = o_quad[:, e * dh:(e + 1) * dh]
            o_h = o_h * _rms_scale(o_h) * nw_ref[...]
            o_ref[:, sl] = (o_h * zs_ref[:, sl].astype(F32)).astype(BF16)


def gdn_mixer_core(act, ba_raw, a_log, dt_bias, norm_w, *, bsz, seq, qk_heads, v_heads):
    t = act.shape[0]
    q = GDN_CHUNK
    rows = q * GDN_CHUNKS_PER_STEP
    assert seq % rows == 0
    nc = seq // rows
    dh = G_HEAD_DIM
    key_dim = qk_heads * dh
    val_dim = v_heads * dh
    conv_dim = 2 * key_dim + val_dim
    assert conv_dim == 2 * val_dim and v_heads % GDN_QUAD == 0
    pad = LANES_V7X - v_heads
    z_blk = conv_dim // val_dim
    row = lambda b, c: b * nc + c
    const = lambda shape: pl.BlockSpec(shape, lambda b, c: (0, 0))
    return pl.pallas_call(
        functools.partial(_gdn_kernel, q=q, qk_heads=qk_heads, v_heads=v_heads),
        grid=(bsz, nc),
        in_specs=[
            pl.BlockSpec((rows, conv_dim), lambda b, c: (row(b, c), 0)),
            pl.BlockSpec((rows, val_dim), lambda b, c: (row(b, c), z_blk)),
            pl.BlockSpec((rows, 2 * LANES_V7X), lambda b, c: (row(b, c), 0)),
            const((1, LANES_V7X)),
            const((1, LANES_V7X)),
            const((1, dh)),
        ],
        out_specs=pl.BlockSpec((rows, val_dim), lambda b, c: (row(b, c), 0)),
        out_shape=jax.ShapeDtypeStruct((t, val_dim), BF16),
        scratch_shapes=[pltpu.VMEM((qk_heads, dh, (v_heads // qk_heads) * dh), F32)],
        compiler_params=_cparams(("parallel", "arbitrary")),
        name="gdn_core",
    )(act, act, ba_raw, jnp.pad(dt_bias, (0, pad)).reshape(1, LANES_V7X),
      jnp.pad(a_log, (0, pad)).reshape(1, LANES_V7X), norm_w.reshape(1, dh))


def _pad_cols(w, width):
    return jnp.pad(w, ((0, 0), (0, width - w.shape[1])))


def _ssd_in_weights(in_w, inner, conv_dim):
    z = in_w[:, :inner]
    xbc = in_w[:, inner:inner + conv_dim]
    dt = _pad_cols(in_w[:, inner + conv_dim:], LANES_V7X)
    return jnp.concatenate([xbc, z], axis=1).astype(BF16), dt.astype(BF16)


def _gdn_in_weights(in_w, conv_dim, val_dim, v_heads):
    main = in_w[:, :conv_dim + val_dim]
    b = _pad_cols(in_w[:, conv_dim + val_dim:conv_dim + val_dim + v_heads], LANES_V7X)
    a = _pad_cols(in_w[:, conv_dim + val_dim + v_heads:], LANES_V7X)
    return main.astype(BF16), jnp.concatenate([b, a], axis=1).astype(BF16)


def _pick_tile(n, candidates):
    for cand in candidates:
        if n % cand == 0:
            return cand
    raise ValueError(f"no tile for {n} in {candidates}")


def kernel(x, mem, ln_mix, ln_xattn, ln_mem, ln_ffn, final_norm, m_in_w, m_conv_w, m_conv_b, m_dt_bias, m_a_log, m_d, m_norm_w, m_out_w, h_in_w, h_lower_bounds, h_norm_w, h_out_w, g_in_w, g_conv_w, g_a_log, g_dt_bias, g_norm_w, g_out_w, xa_q, xa_kv, xa_o, f_up, f_conv_w, f_conv_b, f_down):
    bsz, seq, d = x.shape
    n_mem = mem.shape[1]
    depth = ln_mix.shape[0]
    t = bsz * seq
    tm = _pick_tile(t, (1024, 512, 256, 128, 64))
    tq = _pick_tile(seq, (512, 256, 128, 64))
    ts = _pick_tile(seq, (1024, 512, 256, 128, 64))

    lb = jnp.cumsum(jax.nn.softmax(h_lower_bounds.astype(F32), axis=0), axis=0)
    lb = lb - lb[:1]

    m_inner = m_out_w.shape[1]
    m_heads = m_dt_bias.shape[1]
    m_conv_dim = m_conv_w.shape[2]
    g_v_heads = g_a_log.shape[1]
    g_val_dim = g_out_w.shape[1]
    g_conv_dim = g_conv_w.shape[2]
    g_qk_heads = (g_conv_dim - g_val_dim) // (2 * G_HEAD_DIM)
    d_ff = f_down.shape[1]

    xf = x.reshape(t, d)
    memf = mem.reshape(bsz * n_mem, d)
    ia = ib = ic = 0
    for i in range(depth):
        kind = i % N_MIXERS
        if kind == 0:
            w_in, w_dt = _ssd_in_weights(m_in_w[ia], m_inner, m_conv_dim)
            act, dt_raw = mixer_in_proj(xf, ln_mix[i], w_in, w_dt, m_conv_w[ia], m_conv_b[ia],
                                        bsz=bsz, seq=seq, tm=ts, tn=1024)
            y = ssd_mixer_core(act, dt_raw, m_dt_bias[ia], m_a_log[ia], m_d[ia],
                               m_norm_w[ia], bsz=bsz, seq=seq, inner=m_inner, heads=m_heads)
            xf = matmul_residual(y, m_out_w[ia].astype(BF16), xf, tm=tq)
            ia += 1
        elif kind == 1:
            proj = norm_matmul(xf, ln_mix[i], h_in_w[ib].astype(BF16), tm=tm, tn=1024)
            y = hgrn2_mixer_core(proj, lb[i], h_norm_w[ib], bsz=bsz, seq=seq, d=d)
            xf = matmul_residual(y, h_out_w[ib].astype(BF16), xf, tm=tq)
            ib += 1
        else:
            w_in, w_ba = _gdn_in_weights(g_in_w[ic], g_conv_dim, g_val_dim, g_v_heads)
            act, ba_raw = mixer_in_proj(xf, ln_mix[i], w_in, w_ba, g_conv_w[ic], jnp.zeros((g_conv_dim,), F32),
                                        bsz=bsz, seq=seq, tm=ts, tn=1024)
            y = gdn_mixer_core(act, ba_raw, g_a_log[ic], g_dt_bias[ic], g_norm_w[ic],
                               bsz=bsz, seq=seq, qk_heads=g_qk_heads, v_heads=g_v_heads)
            xf = matmul_residual(y, g_out_w[ic].astype(BF16), xf, tm=tq)
            ic += 1

        kv = norm_matmul(memf, ln_mem[i], xa_kv[i].astype(BF16), tm=_pick_tile(bsz * n_mem, (1024, 512, 256)),
                         tn=1024, out_dtype=BF16)
        xf = memory_cross_attention(xf, ln_xattn[i], xa_q[i].astype(BF16), kv, xa_o[i].astype(BF16),
                                    bsz=bsz, seq=seq, n_mem=n_mem, tq=tq)

        up_w = f_up[i].astype(BF16)
        last = i == depth - 1
        xf = conv_glu_ffn(xf, ln_ffn[i], up_w[:, :d_ff], up_w[:, d_ff:], f_conv_w[i], f_conv_b[i],
                          f_down[i].astype(BF16), bsz=bsz, seq=seq, tm=tq,
                          final_gain=final_norm if last else None)
    return xf.reshape(bsz, seq, d)
```

```python
import functools

import jax
import jax.numpy as jnp
from jax import lax
from jax.experimental import pallas as pl
from jax.experimental.pallas import tpu as pltpu

F32 = jnp.float32
BF16 = jnp.bfloat16
EPS = 1e-6
NEG_INF = float("-inf")

LANES_V7X = 128
SUBLANES_V7X = 8
VMEM_LIMIT_BYTES_V7X = 56 * 1024 * 1024

CONV_W = 4
FFN_CONV_W = 3
HALO = SUBLANES_V7X

M_HEAD_DIM = 64
M_GROUPS = 8
M_STATE = 128
SSD_CHUNK = 128
SSD_CHUNKS_PER_STEP = 4
H_EXPAND = 128
HGRN_CHUNK = 64
HGRN_CHUNKS_PER_STEP = 8
G_HEAD_DIM = 128
GDN_CHUNK = 64
GDN_CHUNKS_PER_STEP = 4
GDN_QUAD = 4
X_HEADS = 4
N_MIXERS = 3


def _cparams(sem):
    return pltpu.CompilerParams(dimension_semantics=sem, vmem_limit_bytes=VMEM_LIMIT_BYTES_V7X)


def _dot(a, b):
    return jnp.dot(a, b, preferred_element_type=F32)


def _dot_nt(a, b):
    return lax.dot_general(a, b, (((1,), (1,)), ((), ())), preferred_element_type=F32)


def _dot_tn(a, b):
    return lax.dot_general(a, b, (((0,), (0,)), ((), ())), preferred_element_type=F32)


def _split3(x):
    hi = x.astype(BF16)
    r1 = x - hi.astype(F32)
    mid = r1.astype(BF16)
    lo = (r1 - mid.astype(F32)).astype(BF16)
    return hi, mid, lo


def _dot_sel_lhs(sel, x):
    parts = _split3(x)
    return _dot(jnp.concatenate([sel] * len(parts), axis=1), jnp.concatenate(parts, axis=0))


def _sigmoid(x):
    return 0.5 + 0.5 * jnp.tanh(0.5 * x)


def _silu(x):
    half = 0.5 * x
    return half + half * jnp.tanh(half)


def _softplus(x):
    return jnp.maximum(x, 0.0) + jnp.log1p(jnp.exp(-jnp.abs(x)))


def _rms_scale(x):
    return lax.rsqrt(jnp.mean(x * x, axis=-1, keepdims=True) + EPS)


def _tril_mask(n):
    row = lax.broadcasted_iota(jnp.int32, (n, n), 0)
    col = lax.broadcasted_iota(jnp.int32, (n, n), 1)
    return row >= col


def _as_sel(mask):
    return jnp.where(mask, 1.0, 0.0).astype(BF16)


def _shift_rows_down(x, history, s):
    rows, n = x.shape
    groups = x.reshape(rows // SUBLANES_V7X, SUBLANES_V7X, n)
    rolled = pltpu.roll(groups, s, axis=1)
    hist_rolled = pltpu.roll(history, s, axis=0).reshape(1, SUBLANES_V7X, n)
    prev = jnp.concatenate([hist_rolled, rolled[:-1]], axis=0)
    sub = lax.broadcasted_iota(jnp.int32, rolled.shape, 1)
    return jnp.where(sub < s, prev, rolled).reshape(rows, n)


def _norm_matmul_kernel(x_ref, g_ref, w_ref, o_ref, xn_ref):
    @pl.when(pl.program_id(1) == 0)
    def _():
        x = x_ref[...]
        xn_ref[...] = (x * _rms_scale(x) * g_ref[...]).astype(BF16)

    o_ref[...] = _dot(xn_ref[...], w_ref[...]).astype(o_ref.dtype)


def norm_matmul(x, gain, w, *, tm, tn, layer=0, out_dtype=F32):
    t, d = x.shape
    n = w.shape[2]
    assert t % tm == 0 and n % tn == 0, (t, tm, n, tn)
    return pl.pallas_call(
        _norm_matmul_kernel,
        grid=(t // tm, n // tn),
        in_specs=[
            pl.BlockSpec((tm, d), lambda i, j: (i, 0)),
            pl.BlockSpec((1, d), lambda i, j: (0, 0)),
            pl.BlockSpec((None, d, tn), lambda i, j: (layer, 0, j)),
        ],
        out_specs=pl.BlockSpec((tm, tn), lambda i, j: (i, j)),
        out_shape=jax.ShapeDtypeStruct((t, n), out_dtype),
        scratch_shapes=[pltpu.VMEM((tm, d), BF16)],
        compiler_params=_cparams(("parallel", "arbitrary")),
        name="norm_matmul",
    )(x, gain.reshape(1, d), w)


def _matmul_res_kernel(a_ref, w_ref, r_ref, o_ref):
    o_ref[...] = r_ref[...] + _dot(a_ref[...], w_ref[...])


def matmul_residual(a, w, res, *, tm, layer=0):
    t, k = a.shape
    d = w.shape[2]
    assert t % tm == 0
    return pl.pallas_call(
        _matmul_res_kernel,
        grid=(t // tm,),
        in_specs=[
            pl.BlockSpec((tm, k), lambda i: (i, 0)),
            pl.BlockSpec((None, k, d), lambda i: (layer, 0, 0)),
            pl.BlockSpec((tm, d), lambda i: (i, 0)),
        ],
        out_specs=pl.BlockSpec((tm, d), lambda i: (i, 0)),
        out_shape=jax.ShapeDtypeStruct((t, d), F32),
        compiler_params=_cparams(("parallel",)),
        name="matmul_residual",
    )(a, w, res)


def _xattn_kernel(x_ref, g_ref, wq_ref, kv_ref, wo_ref, o_ref, *, heads):
    x = x_ref[...]
    d = x.shape[1]
    dh = d // heads
    xn = (x * _rms_scale(x) * g_ref[...]).astype(BF16)
    q = (_dot(xn, wq_ref[...]) * (dh ** -0.5)).astype(BF16)
    outs = []
    for h in range(heads):
        k_h = kv_ref[:, h * dh:(h + 1) * dh]
        v_h = kv_ref[:, d + h * dh:d + (h + 1) * dh]
        s = _dot_nt(q[:, h * dh:(h + 1) * dh], k_h)
        p = jnp.exp(s - jnp.max(s, axis=-1, keepdims=True))
        o_h = _dot(p.astype(BF16), v_h) / jnp.sum(p, axis=-1, keepdims=True)
        outs.append(o_h.astype(BF16))
    o_ref[...] = x + _dot(jnp.concatenate(outs, axis=1), wo_ref[...])


def memory_cross_attention(x, gain, wq, kv, wo, *, layer, bsz, seq, n_mem, tq):
    t, d = x.shape
    nq = seq // tq
    return pl.pallas_call(
        functools.partial(_xattn_kernel, heads=X_HEADS),
        grid=(bsz, nq),
        in_specs=[
            pl.BlockSpec((tq, d), lambda b, i: (b * nq + i, 0)),
            pl.BlockSpec((1, d), lambda b, i: (0, 0)),
            pl.BlockSpec((None, d, d), lambda b, i: (layer, 0, 0)),
            pl.BlockSpec((n_mem, 2 * d), lambda b, i: (b, 0)),
            pl.BlockSpec((None, d, d), lambda b, i: (layer, 0, 0)),
        ],
        out_specs=pl.BlockSpec((tq, d), lambda b, i: (b * nq + i, 0)),
        out_shape=jax.ShapeDtypeStruct((t, d), F32),
        compiler_params=_cparams(("parallel", "parallel")),
        name="memory_xattn",
    )(x, gain.reshape(1, d), wq, kv, wo)


def _ffn_kernel(x_ref, g_ref, wg_ref, wu_ref, cw_ref, cb_ref, wd_ref, *rest, final_norm):
    if final_norm:
        fg_ref, o_ref, act_ref, halo_ref = rest
    else:
        o_ref, act_ref, halo_ref = rest
    tm = x_ref.shape[0]
    f = wg_ref.shape[1]

    @pl.when(pl.program_id(1) == 0)
    def _():
        halo_ref[...] = jnp.zeros_like(halo_ref)

    x = x_ref[...]
    xn = (x * _rms_scale(x) * g_ref[...]).astype(BF16)
    cols_per = 2 * LANES_V7X
    for c in range(0, f, cols_per):
        cols = slice(c, c + cols_per)
        gate = _dot(xn, wg_ref[:, cols])
        up = _dot(xn, wu_ref[:, cols])
        history = halo_ref[:, cols]
        halo_ref[:, cols] = gate[tm - HALO:tm, :]
        conv = cb_ref[:, cols] + cw_ref[FFN_CONV_W - 1:FFN_CONV_W, cols] * gate
        for tap in range(FFN_CONV_W - 1):
            conv = conv + cw_ref[tap:tap + 1, cols] * _shift_rows_down(gate, history, FFN_CONV_W - 1 - tap)
        act_ref[:, cols] = (_silu(conv) * up).astype(BF16)
    y = x + _dot(act_ref[...], wd_ref[...])
    if final_norm:
        y = y * _rms_scale(y) * fg_ref[...]
    o_ref[...] = y


def conv_glu_ffn(x, gain, w_up, conv_w, conv_b, wd, *, layer, bsz, seq, tm, final_gain=None):
    t, d = x.shape
    f = wd.shape[1]
    assert seq % tm == 0 and f % (2 * LANES_V7X) == 0
    ni = seq // tm
    resident = lambda shape: pl.BlockSpec(shape, lambda b, i: (0, 0), pipeline_mode=pl.Buffered(1))
    stacked = lambda shape, col: pl.BlockSpec((None,) + shape, lambda b, i: (layer, 0, col),
                                              pipeline_mode=pl.Buffered(1))
    in_specs = [
        pl.BlockSpec((tm, d), lambda b, i: (b * ni + i, 0)),
        resident((1, d)),
        stacked((d, f), 0),
        stacked((d, f), 1),
        resident((FFN_CONV_W, f)),
        resident((1, f)),
        stacked((f, d), 0),
    ]
    args = [x, gain.reshape(1, d), w_up, w_up, conv_w, conv_b.reshape(1, f), wd]
    if final_gain is not None:
        in_specs.append(resident((1, d)))
        args.append(final_gain.reshape(1, d))
    return pl.pallas_call(
        functools.partial(_ffn_kernel, final_norm=final_gain is not None),
        grid=(bsz, ni),
        in_specs=in_specs,
        out_specs=pl.BlockSpec((tm, d), lambda b, i: (b * ni + i, 0)),
        out_shape=jax.ShapeDtypeStruct((t, d), F32),
        scratch_shapes=[pltpu.VMEM((tm, f), BF16), pltpu.VMEM((HALO, f), F32)],
        compiler_params=_cparams(("parallel", "arbitrary")),
        name="conv_glu_ffn",
    )(*args)


def _inproj_kernel(x_ref, g_ref, w_ref, cw_ref, cb_ref, ws_ref, o_ref, os_ref, xn_ref, gp_ref, halo_ref,
                   *, n_conv_tiles):
    i = pl.program_id(1)
    j = pl.program_id(2)
    tm = x_ref.shape[0]

    @pl.when(j == 0)
    def _():
        x = x_ref[...]
        xn = (x * _rms_scale(x) * g_ref[...]).astype(BF16)
        xn_ref[...] = xn
        os_ref[...] = _dot(xn, ws_ref[...])

    @pl.when(i == 0)
    def _():
        gp_ref[...] = jnp.zeros_like(gp_ref)

    @pl.when(jnp.logical_and(i > 0, j < n_conv_tiles))
    def _():
        gp_ref[...] = halo_ref[j]

    @pl.when(j < n_conv_tiles)
    def _():
        acc = _dot(xn_ref[...], w_ref[...])
        halo_ref[j] = acc[tm - HALO:tm, :]
        conv = cb_ref[...] + cw_ref[CONV_W - 1:CONV_W, :] * acc
        for tap in range(CONV_W - 1):
            conv = conv + cw_ref[tap:tap + 1, :] * _shift_rows_down(acc, gp_ref[...], CONV_W - 1 - tap)
        o_ref[...] = _silu(conv).astype(BF16)

    @pl.when(j >= n_conv_tiles)
    def _():
        o_ref[...] = _silu(_dot(xn_ref[...], w_ref[...])).astype(BF16)


def mixer_in_proj(x, gain, w, w_small, conv_w, conv_b, *, bsz, seq, tm, tn):
    t, d = x.shape
    n = w.shape[1]
    conv_dim = conv_w.shape[1]
    ns = w_small.shape[1]
    assert seq % tm == 0 and n % tn == 0 and conv_dim % tn == 0
    ni, nj, n_conv_tiles = seq // tm, n // tn, conv_dim // tn
    conv_col = lambda b, i, j: (0, jnp.minimum(j, n_conv_tiles - 1))
    return pl.pallas_call(
        functools.partial(_inproj_kernel, n_conv_tiles=n_conv_tiles),
        grid=(bsz, ni, nj),
        in_specs=[
            pl.BlockSpec((tm, d), lambda b, i, j: (b * ni + i, 0)),
            pl.BlockSpec((1, d), lambda b, i, j: (0, 0)),
            pl.BlockSpec((d, tn), lambda b, i, j: (0, j)),
            pl.BlockSpec((CONV_W, tn), conv_col),
            pl.BlockSpec((1, tn), conv_col),
            pl.BlockSpec((d, ns), lambda b, i, j: (0, 0)),
        ],
        out_specs=[
            pl.BlockSpec((tm, tn), lambda b, i, j: (b * ni + i, j)),
            pl.BlockSpec((tm, ns), lambda b, i, j: (b * ni + i, 0)),
        ],
        out_shape=[jax.ShapeDtypeStruct((t, n), BF16), jax.ShapeDtypeStruct((t, ns), F32)],
        scratch_shapes=[
            pltpu.VMEM((tm, d), BF16),
            pltpu.VMEM((HALO, tn), F32),
            pltpu.VMEM((n_conv_tiles, HALO, tn), F32),
        ],
        compiler_params=_cparams(("parallel", "arbitrary", "arbitrary")),
        name="mixer_in_proj",
    )(x, gain.reshape(1, d), w, conv_w, conv_b.reshape(1, conv_dim), w_small)


def _split2(x):
    hi = x.astype(BF16)
    return hi, (x - hi.astype(F32)).astype(BF16)


def _select(parts, sel_stacked):
    k = parts[0].shape[1]
    return _dot(jnp.concatenate(parts, axis=1), sel_stacked[0:len(parts) * k, :])


def _ssd_kernel(xbc_ref, zs_ref, dt_ref, dtb_ref, alog_ref, dskip_ref, nw_ref,
                ehead_ref, o_ref, s_ref, *, q, groups, inner, state):
    @pl.when(pl.program_id(1) == 0)
    def _():
        s_ref[...] = jnp.zeros_like(s_ref)

    for sc in range(xbc_ref.shape[0] // q):
        rows = pl.ds(sc * q, q)
        _ssd_chunk(xbc_ref.at[rows], zs_ref.at[rows], dt_ref.at[rows], dtb_ref, alog_ref, dskip_ref, nw_ref,
                   ehead_ref, o_ref.at[rows], s_ref, q=q, groups=groups, inner=inner, state=state)


def _ssd_chunk(xbc_ref, zs_ref, dt_ref, dtb_ref, alog_ref, dskip_ref, nw_ref,
               ehead_ref, o_ref, s_ref, *, q, groups, inner, state):
    gw = inner // groups
    hpg = gw // M_HEAD_DIM
    dt = _softplus(dt_ref[...] + dtb_ref[...])
    a_neg = -jnp.exp(alog_ref[...])
    tril = _tril_mask(q)
    acum = _dot_sel_lhs(_as_sel(tril), dt * a_neg)
    acum_t = acum.T
    dt3 = _split3(dt)
    ac3 = _split3(acum)
    lane_head = lax.broadcasted_iota(jnp.int32, (q, gw), 1) // M_HEAD_DIM
    head_sel = [jnp.where(lane_head == hh, 1.0, 0.0).astype(BF16) for hh in range(hpg)]

    for g in range(groups):
        xs = xbc_ref[:, g * gw:(g + 1) * gw].astype(F32)
        bm16 = xbc_ref[:, inner + g * state:inner + (g + 1) * state]
        c_lo = inner + groups * state + g * state
        cm16 = xbc_ref[:, c_lo:c_lo + state]
        e_g = ehead_ref[:, g * gw:(g + 1) * gw]
        dt_x = _select(dt3, e_g)
        ac_x = _select(ac3, e_g)
        xdt = xs * dt_x
        xdt16 = xdt.astype(BF16)
        cb = _dot_nt(cm16, bm16)
        lhs, rhs = [], []
        for hh in range(hpg):
            h = g * hpg + hh
            diff = jnp.broadcast_to(acum[:, h:h + 1], (q, q)) - acum_t[h:h + 1, :]
            dec = jnp.exp(jnp.where(tril, diff, NEG_INF))
            lhs.append((cb * dec).astype(BF16))
            rhs.append(xdt16 * head_sel[hh])
        y = _dot(jnp.concatenate(lhs, axis=1), jnp.concatenate(rhs, axis=0))
        s_old = s_ref[g]
        y = y + _dot(cm16, s_old.astype(BF16)) * jnp.exp(ac_x)
        y = y + dskip_ref[:, g * gw:(g + 1) * gw] * xs
        last = ac_x[q - 1:q, :]
        xw = (xdt * jnp.exp(last - ac_x)).astype(BF16)
        s_ref[g] = s_old * jnp.exp(last) + _dot_tn(bm16, xw)
        y = y * zs_ref[:, g * gw:(g + 1) * gw].astype(F32)
        o_ref[:, g * gw:(g + 1) * gw] = (y * _rms_scale(y) * nw_ref[:, g * gw:(g + 1) * gw]).astype(BF16)


def ssd_mixer_core(act, dt_raw, dt_bias, a_log, d_skip, norm_w, *, bsz, seq, inner, heads):
    t = act.shape[0]
    q = SSD_CHUNK
    rows = q * SSD_CHUNKS_PER_STEP
    assert seq % rows == 0
    nc = seq // rows
    conv_dim = inner + 2 * M_GROUPS * M_STATE
    assert conv_dim == 2 * inner and heads <= LANES_V7X
    pad = LANES_V7X - heads
    head_ids = jnp.arange(LANES_V7X)
    n_split = 3
    ehead = (head_ids[:, None] == (jnp.arange(inner) // M_HEAD_DIM)[None, :]).astype(BF16)
    ehead = jnp.concatenate([ehead] * n_split, axis=0)
    z_blk = conv_dim // inner
    row = lambda b, c: b * nc + c
    const = lambda shape: pl.BlockSpec(shape, lambda b, c: (0, 0))
    return pl.pallas_call(
        functools.partial(_ssd_kernel, q=q, groups=M_GROUPS, inner=inner, state=M_STATE),
        grid=(bsz, nc),
        in_specs=[
            pl.BlockSpec((rows, conv_dim), lambda b, c: (row(b, c), 0)),
            pl.BlockSpec((rows, inner), lambda b, c: (row(b, c), z_blk)),
            pl.BlockSpec((rows, LANES_V7X), lambda b, c: (row(b, c), 0)),
            const((1, LANES_V7X)),
            const((1, LANES_V7X)),
            const((1, inner)),
            const((1, inner)),
            const((n_split * LANES_V7X, inner)),
        ],
        out_specs=pl.BlockSpec((rows, inner), lambda b, c: (row(b, c), 0)),
        out_shape=jax.ShapeDtypeStruct((t, inner), BF16),
        scratch_shapes=[pltpu.VMEM((M_GROUPS, M_STATE, inner // M_GROUPS), F32)],
        compiler_params=_cparams(("parallel", "arbitrary")),
        name="ssd_core",
    )(act, act, dt_raw,
      jnp.pad(dt_bias, (0, pad)).reshape(1, LANES_V7X), jnp.pad(a_log, (0, pad)).reshape(1, LANES_V7X),
      jnp.repeat(d_skip, M_HEAD_DIM).reshape(1, inner), norm_w.reshape(1, inner), ehead)


def _hgrn2_kernel(q_ref, f_ref, i_ref, g_ref, lb_ref, nw_ref, o_ref, s_ref, *, q, heads):
    c = pl.program_id(1)
    dk = H_EXPAND

    @pl.when(c == 0)
    def _():
        s_ref[...] = jnp.zeros_like(s_ref)

    lb = lb_ref[...]
    tril = _tril_mask(q)
    tril_sel = _as_sel(tril)
    for sc in range(q_ref.shape[0] // q):
        rows = slice(sc * q, (sc + 1) * q)
        forget = lb + (1.0 - lb) * _sigmoid(f_ref[rows, :])
        gc = _dot_sel_lhs(tril_sel, jnp.log(forget))
        key = 1.0 - forget
        qs = _silu(q_ref[rows, :]) * (dk ** -0.5)
        mid = q // 2 - 1
        g_mid = gc[mid:mid + 1, :]
        g_last = gc[q - 1:q, :]
        q_mid = qs * jnp.exp(gc - g_mid)
        k_mid = key * jnp.exp(g_mid - gc)
        q_dec = q_mid.astype(BF16)
        k_inv = k_mid.astype(BF16)
        q_in = (q_mid * jnp.exp(g_mid)).astype(BF16)
        k_end = (k_mid * jnp.exp(g_last - g_mid)).astype(BF16)
        e_last = jnp.exp(g_last)
        for h in range(heads):
            sl = slice(h * dk, (h + 1) * dk)
            v_h = i_ref[rows, sl].astype(BF16)
            att = jnp.where(tril, _dot_nt(q_dec[:, sl], k_inv[:, sl]), 0.0)
            st_old = s_ref[h]
            o_h = _dot(att.astype(BF16), v_h) + _dot_nt(q_in[:, sl], st_old.astype(BF16))
            s_ref[h] = st_old * e_last[:, sl] + _dot_tn(v_h, k_end[:, sl])
            o_h = o_h * _rms_scale(o_h) * nw_ref[...]
            o_ref[rows, sl] = (o_h * _silu(g_ref[rows, sl])).astype(BF16)


def hgrn2_mixer_core(proj, lower_bound, norm_w, *, bsz, seq, d):
    t = proj.shape[0]
    q = HGRN_CHUNK
    rows = q * HGRN_CHUNKS_PER_STEP
    assert seq % rows == 0
    nc = seq // rows
    heads = d // H_EXPAND
    row = lambda b, c: b * nc + c
    part = lambda k: pl.BlockSpec((rows, d), lambda b, c: (row(b, c), k))
    return pl.pallas_call(
        functools.partial(_hgrn2_kernel, q=q, heads=heads),
        grid=(bsz, nc),
        in_specs=[part(0), part(1), part(2), part(3),
                  pl.BlockSpec((1, d), lambda b, c: (0, 0)),
                  pl.BlockSpec((1, H_EXPAND), lambda b, c: (0, 0))],
        out_specs=pl.BlockSpec((rows, d), lambda b, c: (row(b, c), 0)),
        out_shape=jax.ShapeDtypeStruct((t, d), BF16),
        scratch_shapes=[pltpu.VMEM((heads, H_EXPAND, d // heads), F32)],
        compiler_params=_cparams(("parallel", "arbitrary")),
        name="hgrn2_core",
    )(proj, proj, proj, proj, lower_bound.reshape(1, d), norm_w.reshape(1, H_EXPAND))


def _block_diag(blocks):
    n = len(blocks)
    rows = []
    for e, blk in enumerate(blocks):
        zero = jnp.zeros_like(blk)
        rows.append(jnp.concatenate([blk if k == e else zero for k in range(n)], axis=1))
    return jnp.concatenate(rows, axis=0)


def _gdn_kernel(qkv_ref, zs_ref, ba_ref, dtb_ref, alog_ref, nw_ref, o_ref, s_ref, *, q, qk_heads, v_heads):
    @pl.when(pl.program_id(1) == 0)
    def _():
        s_ref[...] = jnp.zeros_like(s_ref)

    for sc in range(qkv_ref.shape[0] // q):
        rows = pl.ds(sc * q, q)
        _gdn_chunk(qkv_ref.at[rows], zs_ref.at[rows], ba_ref.at[rows], dtb_ref, alog_ref, nw_ref,
                   o_ref.at[rows], s_ref, q=q, qk_heads=qk_heads, v_heads=v_heads)


def _gdn_chunk(qkv_ref, zs_ref, ba_ref, dtb_ref, alog_ref, nw_ref, o_ref, s_ref, *, q, qk_heads, v_heads):
    dh = G_HEAD_DIM
    key_dim = qk_heads * dh
    rep = v_heads // qk_heads
    assert rep == 2 and GDN_QUAD % rep == 0 and 2 * q == LANES_V7X
    tril = _tril_mask(q)
    beta = _sigmoid(ba_ref[:, 0:LANES_V7X])
    gate = -jnp.exp(alog_ref[...]) * _softplus(ba_ref[:, LANES_V7X:2 * LANES_V7X] + dtb_ref[...])
    gc = _dot_sel_lhs(_as_sel(tril), gate)

    eye_t = (lax.broadcasted_iota(jnp.int32, (q, v_heads * q), 0)
             == lax.broadcasted_iota(jnp.int32, (q, v_heads * q), 1) % q)
    col_s = lax.broadcasted_iota(jnp.int32, (q, v_heads * q), 1) % q
    row_l = lax.broadcasted_iota(jnp.int32, (q, v_heads * q), 0)
    lane_lo = lax.broadcasted_iota(jnp.int32, (q, 2 * q), 1) < q
    g_wide = [jnp.broadcast_to(gc[:, h:h + 1], (q, dh)) for h in range(v_heads)]
    b_wide = [jnp.broadcast_to(beta[:, h:h + 1], (q, dh)) for h in range(v_heads)]
    g_col = jnp.concatenate([jnp.where(lane_lo, g_wide[h], g_wide[h + 1]) for h in range(0, v_heads, 2)], axis=1)
    b_col = jnp.concatenate([jnp.where(lane_lo, b_wide[h], b_wide[h + 1]) for h in range(0, v_heads, 2)], axis=1)
    g_row = jnp.sum(jnp.where(eye_t, g_col, 0.0), axis=0, keepdims=True)
    decay = jnp.exp(jnp.where(row_l >= col_s, g_col - g_row, NEG_INF))

    kk_parts, qk_parts, k_heads, k_f32, q_heads = [], [], [], [], []
    for p in range(qk_heads):
        q_p = qkv_ref[:, p * dh:(p + 1) * dh].astype(F32)
        k_p = qkv_ref[:, key_dim + p * dh:key_dim + (p + 1) * dh].astype(F32)
        q_p = q_p * lax.rsqrt(jnp.sum(q_p * q_p, axis=-1, keepdims=True) + EPS) * (dh ** -0.5)
        k_p = k_p * lax.rsqrt(jnp.sum(k_p * k_p, axis=-1, keepdims=True) + EPS)
        k16 = k_p.astype(BF16)
        q16 = q_p.astype(BF16)
        k_rep = jnp.concatenate([k16] * rep, axis=0)
        kk_parts.append(_dot_nt(k16, k_rep))
        qk_parts.append(_dot_nt(q16, k_rep))
        k_heads.append(k16)
        k_f32.append(k_p)
        q_heads.append(q16)
    kk = jnp.concatenate(kk_parts, axis=1)
    qk = jnp.concatenate(qk_parts, axis=1)
    m_all = jnp.where(row_l > col_s, b_col * kk * decay, 0.0)
    att_all = qk * decay

    quad_w = GDN_QUAD * q
    n_quads = v_heads // GDN_QUAD
    lane_blk = lax.broadcasted_iota(jnp.int32, (q, quad_w), 1) // q
    blk_sel = [jnp.where(lane_blk == e, 1.0, 0.0).astype(BF16) for e in range(GDN_QUAD)]

    def blocks_on_diagonal(x16):
        return jnp.concatenate([x16 * blk_sel[e] for e in range(GDN_QUAD)], axis=0)

    def times_blockwise(lhs_hi, lhs_lo, bd_hi, bd_lo):
        lhs = jnp.concatenate([jnp.concatenate([hi, lo, hi], axis=1) for hi, lo in zip(lhs_hi, lhs_lo)], axis=0)
        out = _dot(lhs, jnp.concatenate([bd_hi, bd_hi, bd_lo], axis=0))
        return [out[i * q:(i + 1) * q, :] for i in range(len(lhs_hi))]

    levels = q.bit_length() - 2
    p_cur = [-m_all[:, u * quad_w:(u + 1) * quad_w] for u in range(n_quads)]
    n_acc = list(p_cur)
    for lvl in range(levels + 1):
        for u in range(n_quads):
            p_hi, p_lo = _split2(p_cur[u])
            bd_hi, bd_lo = blocks_on_diagonal(p_hi), blocks_on_diagonal(p_lo)
            if lvl == 0:
                (p_cur[u],) = times_blockwise([p_hi], [p_lo], bd_hi, bd_lo)
            elif lvl < levels:
                n_hi, n_lo = _split2(n_acc[u])
                n_p, p_sq = times_blockwise([n_hi, p_hi], [n_lo, p_lo], bd_hi, bd_lo)
                n_acc[u] = n_acc[u] + p_cur[u] + n_p
                p_cur[u] = p_sq
            else:
                n_hi, n_lo = _split2(n_acc[u])
                (n_p,) = times_blockwise([n_hi], [n_lo], bd_hi, bd_lo)
                n_acc[u] = n_acc[u] + p_cur[u] + n_p

    def two_pass(a, rhs16):
        a_hi, a_lo = _split2(a)
        return _dot(jnp.concatenate([a_hi, a_lo], axis=1), jnp.concatenate([rhs16, rhs16], axis=0))

    b_row = jnp.sum(jnp.where(eye_t, b_col, 0.0), axis=0, keepdims=True)
    bg_row = b_row * jnp.exp(g_row)
    eye_f = jnp.where(eye_t, 1.0, 0.0)
    zero_blk = jnp.zeros((dh, dh), BF16)
    pairs_per_quad = GDN_QUAD // rep
    for u in range(n_quads):
        cols = slice(u * quad_w, (u + 1) * quad_w)
        t_b = (n_acc[u] + eye_f[:, cols]) * b_row[:, cols]
        t_bg = (n_acc[u] + eye_f[:, cols]) * bg_row[:, cols]
        v16 = [qkv_ref[:, 2 * key_dim + h * dh:2 * key_dim + (h + 1) * dh]
               for h in range(u * GDN_QUAD, (u + 1) * GDN_QUAD)]
        k16 = [k_heads[h // rep] for h in range(u * GDN_QUAD, (u + 1) * GDN_QUAD)]
        u_quad = two_pass(t_b, _block_diag(v16))
        w_quad = two_pass(t_bg, _block_diag(k16))
        v_new, o_inter = [], []
        for pi in range(pairs_per_quad):
            p = u * pairs_per_quad + pi
            pc = slice(pi * rep * dh, (pi + 1) * rep * dh)
            g_pair = jnp.concatenate(g_wide[p * rep:(p + 1) * rep], axis=1)
            g_last = g_pair[q - 1:q, :]
            s_old = s_ref[p]
            s16 = s_old.astype(BF16)
            s_bd = jnp.concatenate([jnp.concatenate([s16[:, 0:dh], zero_blk], axis=1),
                                    jnp.concatenate([zero_blk, s16[:, dh:2 * dh]], axis=1)], axis=0)
            lhs = jnp.concatenate([w_quad[:, pc].astype(BF16), jnp.concatenate([q_heads[p]] * rep, axis=1)],
                                  axis=0)
            both = _dot(lhs, s_bd)
            vn = u_quad[:, pc] - both[0:q, :]
            o_inter.append(both[q:2 * q, :] * jnp.exp(g_pair))
            vn16 = vn.astype(BF16)
            zero_q = jnp.zeros((q, dh), BF16)
            vn_bd = jnp.concatenate([jnp.concatenate([vn16[:, 0:dh], zero_q], axis=1),
                                     jnp.concatenate([zero_q, vn16[:, dh:2 * dh]], axis=1)], axis=0)
            k_dec = jnp.concatenate([k_f32[p]] * rep, axis=1) * jnp.exp(g_last - g_pair)
            ke = jnp.concatenate([k_dec[:, 0:dh], k_dec[:, dh:2 * dh]], axis=0).astype(BF16)
            s_ref[p] = s_old * jnp.exp(g_last) + _dot_tn(ke, vn_bd)
            v_new.extend([vn16[:, 0:dh], vn16[:, dh:2 * dh]])
        o_intra = _dot(att_all[:, cols].astype(BF16), _block_diag(v_new))
        o_quad = jnp.concatenate(o_inter, axis=1) + o_intra
        for e in range(GDN_QUAD):
            h = u * GDN_QUAD + e
            sl = slice(h * dh, (h + 1) * dh)
            o_h = o_quad[:, e * dh:(e + 1) * dh]
            o_h = o_h * _rms_scale(o_h) * nw_ref[...]
            o_ref[:, sl] = (o_h * zs_ref[:, sl].astype(F32)).astype(BF16)


def gdn_mixer_core(act, ba_raw, a_log, dt_bias, norm_w, *, bsz, seq, qk_heads, v_heads):
    t = act.shape[0]
    q = GDN_CHUNK
    rows = q * GDN_CHUNKS_PER_STEP
    assert seq % rows == 0
    nc = seq // rows
    dh = G_HEAD_DIM
    key_dim = qk_heads * dh
    val_dim = v_heads * dh
    conv_dim = 2 * key_dim + val_dim
    assert conv_dim == 2 * val_dim and v_heads % GDN_QUAD == 0
    pad = LANES_V7X - v_heads
    z_blk = conv_dim // val_dim
    row = lambda b, c: b * nc + c
    const = lambda shape: pl.BlockSpec(shape, lambda b, c: (0, 0))
    return pl.pallas_call(
        functools.partial(_gdn_kernel, q=q, qk_heads=qk_heads, v_heads=v_heads),
        grid=(bsz, nc),
        in_specs=[
            pl.BlockSpec((rows, conv_dim), lambda b, c: (row(b, c), 0)),
            pl.BlockSpec((rows, val_dim), lambda b, c: (row(b, c), z_blk)),
            pl.BlockSpec((rows, 2 * LANES_V7X), lambda b, c: (row(b, c), 0)),
            const((1, LANES_V7X)),
            const((1, LANES_V7X)),
            const((1, dh)),
        ],
        out_specs=pl.BlockSpec((rows, val_dim), lambda b, c: (row(b, c), 0)),
        out_shape=jax.ShapeDtypeStruct((t, val_dim), BF16),
        scratch_shapes=[pltpu.VMEM((qk_heads, dh, (v_heads // qk_heads) * dh), F32)],
        compiler_params=_cparams(("parallel", "arbitrary")),
        name="gdn_core",
    )(act, act, ba_raw, jnp.pad(dt_bias, (0, pad)).reshape(1, LANES_V7X),
      jnp.pad(a_log, (0, pad)).reshape(1, LANES_V7X), norm_w.reshape(1, dh))


def _pad_cols(w, width):
    return jnp.pad(w, ((0, 0), (0, width - w.shape[1])))


def _ssd_in_weights(in_w, inner, conv_dim):
    z = in_w[:, :inner]
    xbc = in_w[:, inner:inner + conv_dim]
    dt = _pad_cols(in_w[:, inner + conv_dim:], LANES_V7X)
    return jnp.concatenate([xbc, z], axis=1).astype(BF16), dt.astype(BF16)


def _gdn_in_weights(in_w, conv_dim, val_dim, v_heads):
    main = in_w[:, :conv_dim + val_dim]
    b = _pad_cols(in_w[:, conv_dim + val_dim:conv_dim + val_dim + v_heads], LANES_V7X)
    a = _pad_cols(in_w[:, conv_dim + val_dim + v_heads:], LANES_V7X)
    return main.astype(BF16), jnp.concatenate([b, a], axis=1).astype(BF16)


def _pick_tile(n, candidates):
    for cand in candidates:
        if n % cand == 0:
            return cand
    raise ValueError(f"no tile for {n} in {candidates}")


def kernel(x, mem, ln_mix, ln_xattn, ln_mem, ln_ffn, final_norm, m_in_w, m_conv_w, m_conv_b, m_dt_bias, m_a_log, m_d, m_norm_w, m_out_w, h_in_w, h_lower_bounds, h_norm_w, h_out_w, g_in_w, g_conv_w, g_a_log, g_dt_bias, g_norm_w, g_out_w, xa_q, xa_kv, xa_o, f_up, f_conv_w, f_conv_b, f_down):
    bsz, seq, d = x.shape
    n_mem = mem.shape[1]
    depth = ln_mix.shape[0]
    t = bsz * seq
    tm = _pick_tile(t, (1024, 512, 256, 128, 64))
    tq = _pick_tile(seq, (512, 256, 128, 64))
    ts = _pick_tile(seq, (1024, 512, 256, 128, 64))

    lb = jnp.cumsum(jax.nn.softmax(h_lower_bounds.astype(F32), axis=0), axis=0)
    lb = lb - lb[:1]

    m_inner = m_out_w.shape[1]
    m_heads = m_dt_bias.shape[1]
    m_conv_dim = m_conv_w.shape[2]
    g_v_heads = g_a_log.shape[1]
    g_val_dim = g_out_w.shape[1]
    g_conv_dim = g_conv_w.shape[2]
    g_qk_heads = (g_conv_dim - g_val_dim) // (2 * G_HEAD_DIM)

    h_in16, xa_q16, xa_kv16, xa_o16 = (w.astype(BF16) for w in (h_in_w, xa_q, xa_kv, xa_o))
    f_up16, f_down16 = f_up.astype(BF16), f_down.astype(BF16)
    m_out16, h_out16, g_out16 = m_out_w.astype(BF16), h_out_w.astype(BF16), g_out_w.astype(BF16)

    xf = x.reshape(t, d)
    memf = mem.reshape(bsz * n_mem, d)
    ia = ib = ic = 0
    for i in range(depth):
        kind = i % N_MIXERS
        if kind == 0:
            w_in, w_dt = _ssd_in_weights(m_in_w[ia], m_inner, m_conv_dim)
            act, dt_raw = mixer_in_proj(xf, ln_mix[i], w_in, w_dt, m_conv_w[ia], m_conv_b[ia],
                                        bsz=bsz, seq=seq, tm=ts, tn=1024)
            y = ssd_mixer_core(act, dt_raw, m_dt_bias[ia], m_a_log[ia], m_d[ia],
                               m_norm_w[ia], bsz=bsz, seq=seq, inner=m_inner, heads=m_heads)
            xf = matmul_residual(y, m_out16, xf, tm=tq, layer=ia)
            ia += 1
        elif kind == 1:
            proj = norm_matmul(xf, ln_mix[i], h_in16, layer=ib, tm=tm, tn=1024)
            y = hgrn2_mixer_core(proj, lb[i], h_norm_w[ib], bsz=bsz, seq=seq, d=d)
            xf = matmul_residual(y, h_out16, xf, tm=tq, layer=ib)
            ib += 1
        else:
            w_in, w_ba = _gdn_in_weights(g_in_w[ic], g_conv_dim, g_val_dim, g_v_heads)
            act, ba_raw = mixer_in_proj(xf, ln_mix[i], w_in, w_ba, g_conv_w[ic], jnp.zeros((g_conv_dim,), F32),
                                        bsz=bsz, seq=seq, tm=ts, tn=1024)
            y = gdn_mixer_core(act, ba_raw, g_a_log[ic], g_dt_bias[ic], g_norm_w[ic],
                               bsz=bsz, seq=seq, qk_heads=g_qk_heads, v_heads=g_v_heads)
            xf = matmul_residual(y, g_out16, xf, tm=tq, layer=ic)
            ic += 1

        kv = norm_matmul(memf, ln_mem[i], xa_kv16, layer=i, tm=_pick_tile(bsz * n_mem, (1024, 512, 256)),
                         tn=1024, out_dtype=BF16)
        xf = memory_cross_attention(xf, ln_xattn[i], xa_q16, kv, xa_o16, layer=i,
                                    bsz=bsz, seq=seq, n_mem=n_mem, tq=tq)

        last = i == depth - 1
        xf = conv_glu_ffn(xf, ln_ffn[i], f_up16, f_conv_w[i], f_conv_b[i], f_down16, layer=i,
                          bsz=bsz, seq=seq, tm=tq, final_gain=final_norm if last else None)
    return xf.reshape(bsz, seq, d)
```

```python
import functools

import jax
import jax.numpy as jnp
from jax import lax
from jax.experimental import pallas as pl
from jax.experimental.pallas import tpu as pltpu

F32 = jnp.float32
BF16 = jnp.bfloat16
EPS = 1e-6
NEG_INF = float("-inf")

LANES_V7X = 128
SUBLANES_V7X = 8
VMEM_LIMIT_BYTES_V7X = 56 * 1024 * 1024

CONV_W = 4
FFN_CONV_W = 3
HALO = SUBLANES_V7X

M_HEAD_DIM = 64
M_GROUPS = 8
M_STATE = 128
SSD_CHUNK = 128
SSD_CHUNKS_PER_STEP = 8
H_EXPAND = 128
HGRN_CHUNK = 64
HGRN_CHUNKS_PER_STEP = 8
G_HEAD_DIM = 128
GDN_CHUNK = 64
GDN_CHUNKS_PER_STEP = 8
GDN_QUAD = 4
X_HEADS = 4
N_MIXERS = 3


def _cparams(sem):
    return pltpu.CompilerParams(dimension_semantics=sem, vmem_limit_bytes=VMEM_LIMIT_BYTES_V7X)


def _dot(a, b):
    return jnp.dot(a, b, preferred_element_type=F32)


def _dot_nt(a, b):
    return lax.dot_general(a, b, (((1,), (1,)), ((), ())), preferred_element_type=F32)


def _dot_tn(a, b):
    return lax.dot_general(a, b, (((0,), (0,)), ((), ())), preferred_element_type=F32)


def _split3(x):
    hi = x.astype(BF16)
    r1 = x - hi.astype(F32)
    mid = r1.astype(BF16)
    lo = (r1 - mid.astype(F32)).astype(BF16)
    return hi, mid, lo


def _dot_sel_lhs(sel, x):
    parts = _split3(x)
    return _dot(jnp.concatenate([sel] * len(parts), axis=1), jnp.concatenate(parts, axis=0))


def _sigmoid(x):
    return 0.5 + 0.5 * jnp.tanh(0.5 * x)


def _silu(x):
    half = 0.5 * x
    return half + half * jnp.tanh(half)


def _softplus(x):
    return jnp.maximum(x, 0.0) + jnp.log1p(jnp.exp(-jnp.abs(x)))


def _rms_scale(x):
    return lax.rsqrt(jnp.mean(x * x, axis=-1, keepdims=True) + EPS)


def _tril_mask(n):
    row = lax.broadcasted_iota(jnp.int32, (n, n), 0)
    col = lax.broadcasted_iota(jnp.int32, (n, n), 1)
    return row >= col


def _as_sel(mask):
    return jnp.where(mask, 1.0, 0.0).astype(BF16)


def _shift_rows_down(x, history, s):
    rows, n = x.shape
    groups = x.reshape(rows // SUBLANES_V7X, SUBLANES_V7X, n)
    rolled = pltpu.roll(groups, s, axis=1)
    hist_rolled = pltpu.roll(history, s, axis=0).reshape(1, SUBLANES_V7X, n)
    prev = jnp.concatenate([hist_rolled, rolled[:-1]], axis=0)
    sub = lax.broadcasted_iota(jnp.int32, rolled.shape, 1)
    return jnp.where(sub < s, prev, rolled).reshape(rows, n)


def _norm_matmul_kernel(x_ref, g_ref, w_ref, o_ref, xn_ref):
    @pl.when(pl.program_id(1) == 0)
    def _():
        x = x_ref[...]
        xn_ref[...] = (x * _rms_scale(x) * g_ref[...]).astype(BF16)

    o_ref[...] = _dot(xn_ref[...], w_ref[...]).astype(o_ref.dtype)


def norm_matmul(x, gain, w, *, tm, tn, layer=0, out_dtype=F32):
    t, d = x.shape
    n = w.shape[2]
    assert t % tm == 0 and n % tn == 0, (t, tm, n, tn)
    return pl.pallas_call(
        _norm_matmul_kernel,
        grid=(t // tm, n // tn),
        in_specs=[
            pl.BlockSpec((tm, d), lambda i, j: (i, 0)),
            pl.BlockSpec((1, d), lambda i, j: (0, 0)),
            pl.BlockSpec((None, d, tn), lambda i, j: (layer, 0, j)),
        ],
        out_specs=pl.BlockSpec((tm, tn), lambda i, j: (i, j)),
        out_shape=jax.ShapeDtypeStruct((t, n), out_dtype),
        scratch_shapes=[pltpu.VMEM((tm, d), BF16)],
        compiler_params=_cparams(("parallel", "arbitrary")),
        name="norm_matmul",
    )(x, gain.reshape(1, d), w)


def _norm_matmul_split_kernel(x_ref, g_ref, w_ref, o16_ref, o32_ref, xn_ref, *, n16_tiles):
    j = pl.program_id(1)

    @pl.when(j == 0)
    def _():
        x = x_ref[...]
        xn_ref[...] = (x * _rms_scale(x) * g_ref[...]).astype(BF16)

    @pl.when(j < n16_tiles)
    def _():
        o16_ref[...] = _dot(xn_ref[...], w_ref[...]).astype(BF16)

    @pl.when(j >= n16_tiles)
    def _():
        o32_ref[...] = _dot(xn_ref[...], w_ref[...])


def norm_matmul_split(x, gain, w, *, n16, tm, tn, layer=0):
    t, d = x.shape
    n = w.shape[2]
    assert t % tm == 0 and n16 % tn == 0 and (n - n16) % tn == 0
    n16_tiles = n16 // tn
    return pl.pallas_call(
        functools.partial(_norm_matmul_split_kernel, n16_tiles=n16_tiles),
        grid=(t // tm, n // tn),
        in_specs=[
            pl.BlockSpec((tm, d), lambda i, j: (i, 0)),
            pl.BlockSpec((1, d), lambda i, j: (0, 0)),
            pl.BlockSpec((None, d, tn), lambda i, j: (layer, 0, j)),
        ],
        out_specs=[
            pl.BlockSpec((tm, tn), lambda i, j: (i, jnp.minimum(j, n16_tiles - 1))),
            pl.BlockSpec((tm, tn), lambda i, j: (i, jnp.maximum(j - n16_tiles, 0))),
        ],
        out_shape=[jax.ShapeDtypeStruct((t, n16), BF16), jax.ShapeDtypeStruct((t, n - n16), F32)],
        scratch_shapes=[pltpu.VMEM((tm, d), BF16)],
        compiler_params=_cparams(("parallel", "arbitrary")),
        name="norm_matmul_split",
    )(x, gain.reshape(1, d), w)


def _matmul_res_kernel(a_ref, w_ref, r_ref, o_ref):
    o_ref[...] = r_ref[...] + _dot(a_ref[...], w_ref[...])


def matmul_residual(a, w, res, *, tm, layer=0):
    t, k = a.shape
    d = w.shape[2]
    assert t % tm == 0
    return pl.pallas_call(
        _matmul_res_kernel,
        grid=(t // tm,),
        in_specs=[
            pl.BlockSpec((tm, k), lambda i: (i, 0)),
            pl.BlockSpec((None, k, d), lambda i: (layer, 0, 0)),
            pl.BlockSpec((tm, d), lambda i: (i, 0)),
        ],
        out_specs=pl.BlockSpec((tm, d), lambda i: (i, 0)),
        out_shape=jax.ShapeDtypeStruct((t, d), F32),
        compiler_params=_cparams(("parallel",)),
        name="matmul_residual",
    )(a, w, res)


def _xattn_kernel(x_ref, g_ref, wq_ref, kv_ref, wo_ref, o_ref, *, heads):
    x = x_ref[...]
    d = x.shape[1]
    dh = d // heads
    xn = (x * _rms_scale(x) * g_ref[...]).astype(BF16)
    q = (_dot(xn, wq_ref[...]) * (dh ** -0.5)).astype(BF16)
    outs = []
    for h in range(heads):
        k_h = kv_ref[:, h * dh:(h + 1) * dh]
        v_h = kv_ref[:, d + h * dh:d + (h + 1) * dh]
        s = _dot_nt(q[:, h * dh:(h + 1) * dh], k_h)
        p = jnp.exp(s - jnp.max(s, axis=-1, keepdims=True))
        o_h = _dot(p.astype(BF16), v_h) / jnp.sum(p, axis=-1, keepdims=True)
        outs.append(o_h.astype(BF16))
    o_ref[...] = x + _dot(jnp.concatenate(outs, axis=1), wo_ref[...])


def memory_cross_attention(x, gain, wq, kv, wo, *, layer, bsz, seq, n_mem, tq):
    t, d = x.shape
    nq = seq // tq
    return pl.pallas_call(
        functools.partial(_xattn_kernel, heads=X_HEADS),
        grid=(bsz, nq),
        in_specs=[
            pl.BlockSpec((tq, d), lambda b, i: (b * nq + i, 0)),
            pl.BlockSpec((1, d), lambda b, i: (0, 0)),
            pl.BlockSpec((None, d, d), lambda b, i: (layer, 0, 0)),
            pl.BlockSpec((n_mem, 2 * d), lambda b, i: (b, 0)),
            pl.BlockSpec((None, d, d), lambda b, i: (layer, 0, 0)),
        ],
        out_specs=pl.BlockSpec((tq, d), lambda b, i: (b * nq + i, 0)),
        out_shape=jax.ShapeDtypeStruct((t, d), F32),
        compiler_params=_cparams(("parallel", "parallel")),
        name="memory_xattn",
    )(x, gain.reshape(1, d), wq, kv, wo)


def _ffn_kernel(x_ref, g_ref, wg_ref, wu_ref, cw_ref, cb_ref, wd_ref, *rest, final_norm):
    if final_norm:
        fg_ref, o_ref, act_ref, halo_ref = rest
    else:
        o_ref, act_ref, halo_ref = rest
    tm = x_ref.shape[0]
    f = wg_ref.shape[1]

    @pl.when(pl.program_id(1) == 0)
    def _():
        halo_ref[...] = jnp.zeros_like(halo_ref)

    x = x_ref[...]
    xn = (x * _rms_scale(x) * g_ref[...]).astype(BF16)
    cols_per = 2 * LANES_V7X
    for c in range(0, f, cols_per):
        cols = slice(c, c + cols_per)
        gate = _dot(xn, wg_ref[:, cols])
        up = _dot(xn, wu_ref[:, cols])
        history = halo_ref[:, cols]
        halo_ref[:, cols] = gate[tm - HALO:tm, :]
        conv = cb_ref[:, cols] + cw_ref[FFN_CONV_W - 1:FFN_CONV_W, cols] * gate
        for tap in range(FFN_CONV_W - 1):
            conv = conv + cw_ref[tap:tap + 1, cols] * _shift_rows_down(gate, history, FFN_CONV_W - 1 - tap)
        act_ref[:, cols] = (_silu(conv) * up).astype(BF16)
    y = x + _dot(act_ref[...], wd_ref[...])
    if final_norm:
        y = y * _rms_scale(y) * fg_ref[...]
    o_ref[...] = y


def conv_glu_ffn(x, gain, w_up, conv_w, conv_b, wd, *, layer, bsz, seq, tm, final_gain=None):
    t, d = x.shape
    f = wd.shape[1]
    assert seq % tm == 0 and f % (2 * LANES_V7X) == 0
    ni = seq // tm
    resident = lambda shape: pl.BlockSpec(shape, lambda b, i: (0, 0), pipeline_mode=pl.Buffered(1))
    stacked = lambda shape, col: pl.BlockSpec((None,) + shape, lambda b, i: (layer, 0, col),
                                              pipeline_mode=pl.Buffered(1))
    in_specs = [
        pl.BlockSpec((tm, d), lambda b, i: (b * ni + i, 0)),
        resident((1, d)),
        stacked((d, f), 0),
        stacked((d, f), 1),
        resident((FFN_CONV_W, f)),
        resident((1, f)),
        stacked((f, d), 0),
    ]
    args = [x, gain.reshape(1, d), w_up, w_up, conv_w, conv_b.reshape(1, f), wd]
    if final_gain is not None:
        in_specs.append(resident((1, d)))
        args.append(final_gain.reshape(1, d))
    return pl.pallas_call(
        functools.partial(_ffn_kernel, final_norm=final_gain is not None),
        grid=(bsz, ni),
        in_specs=in_specs,
        out_specs=pl.BlockSpec((tm, d), lambda b, i: (b * ni + i, 0)),
        out_shape=jax.ShapeDtypeStruct((t, d), F32),
        scratch_shapes=[pltpu.VMEM((tm, f), BF16), pltpu.VMEM((HALO, f), F32)],
        compiler_params=_cparams(("parallel", "arbitrary")),
        name="conv_glu_ffn",
    )(*args)


def _inproj_kernel(x_ref, g_ref, w_ref, cw_ref, cb_ref, ws_ref, o_ref, os_ref, xn_ref, gp_ref, halo_ref,
                   *, n_conv_tiles):
    i = pl.program_id(1)
    j = pl.program_id(2)
    tm = x_ref.shape[0]

    @pl.when(j == 0)
    def _():
        x = x_ref[...]
        xn = (x * _rms_scale(x) * g_ref[...]).astype(BF16)
        xn_ref[...] = xn
        os_ref[...] = _dot(xn, ws_ref[...])

    @pl.when(i == 0)
    def _():
        gp_ref[...] = jnp.zeros_like(gp_ref)

    @pl.when(jnp.logical_and(i > 0, j < n_conv_tiles))
    def _():
        gp_ref[...] = halo_ref[j]

    @pl.when(j < n_conv_tiles)
    def _():
        acc = _dot(xn_ref[...], w_ref[...])
        halo_ref[j] = acc[tm - HALO:tm, :]
        conv = cb_ref[...] + cw_ref[CONV_W - 1:CONV_W, :] * acc
        for tap in range(CONV_W - 1):
            conv = conv + cw_ref[tap:tap + 1, :] * _shift_rows_down(acc, gp_ref[...], CONV_W - 1 - tap)
        o_ref[...] = _silu(conv).astype(BF16)

    @pl.when(j >= n_conv_tiles)
    def _():
        o_ref[...] = _silu(_dot(xn_ref[...], w_ref[...])).astype(BF16)


def mixer_in_proj(x, gain, w, w_small, conv_w, conv_b, *, bsz, seq, tm, tn):
    t, d = x.shape
    n = w.shape[1]
    conv_dim = conv_w.shape[1]
    ns = w_small.shape[1]
    assert seq % tm == 0 and n % tn == 0 and conv_dim % tn == 0
    ni, nj, n_conv_tiles = seq // tm, n // tn, conv_dim // tn
    conv_col = lambda b, i, j: (0, jnp.minimum(j, n_conv_tiles - 1))
    return pl.pallas_call(
        functools.partial(_inproj_kernel, n_conv_tiles=n_conv_tiles),
        grid=(bsz, ni, nj),
        in_specs=[
            pl.BlockSpec((tm, d), lambda b, i, j: (b * ni + i, 0)),
            pl.BlockSpec((1, d), lambda b, i, j: (0, 0)),
            pl.BlockSpec((d, tn), lambda b, i, j: (0, j)),
            pl.BlockSpec((CONV_W, tn), conv_col),
            pl.BlockSpec((1, tn), conv_col),
            pl.BlockSpec((d, ns), lambda b, i, j: (0, 0)),
        ],
        out_specs=[
            pl.BlockSpec((tm, tn), lambda b, i, j: (b * ni + i, j)),
            pl.BlockSpec((tm, ns), lambda b, i, j: (b * ni + i, 0)),
        ],
        out_shape=[jax.ShapeDtypeStruct((t, n), BF16), jax.ShapeDtypeStruct((t, ns), F32)],
        scratch_shapes=[
            pltpu.VMEM((tm, d), BF16),
            pltpu.VMEM((HALO, tn), F32),
            pltpu.VMEM((n_conv_tiles, HALO, tn), F32),
        ],
        compiler_params=_cparams(("parallel", "arbitrary", "arbitrary")),
        name="mixer_in_proj",
    )(x, gain.reshape(1, d), w, conv_w, conv_b.reshape(1, conv_dim), w_small)


def _split2(x):
    hi = x.astype(BF16)
    return hi, (x - hi.astype(F32)).astype(BF16)


def _select(parts, sel_stacked):
    k = parts[0].shape[1]
    return _dot(jnp.concatenate(parts, axis=1), sel_stacked[0:len(parts) * k, :])


def _ssd_kernel(xbc_ref, zs_ref, dt_ref, dtb_ref, alog_ref, dskip_ref, nw_ref,
                ehead_ref, o_ref, s_ref, *, q, groups, inner, state):
    @pl.when(pl.program_id(1) == 0)
    def _():
        s_ref[...] = jnp.zeros_like(s_ref)

    for sc in range(xbc_ref.shape[0] // q):
        rows = pl.ds(sc * q, q)
        _ssd_chunk(xbc_ref.at[rows], zs_ref.at[rows], dt_ref.at[rows], dtb_ref, alog_ref, dskip_ref, nw_ref,
                   ehead_ref, o_ref.at[rows], s_ref, q=q, groups=groups, inner=inner, state=state)


def _ssd_chunk(xbc_ref, zs_ref, dt_ref, dtb_ref, alog_ref, dskip_ref, nw_ref,
               ehead_ref, o_ref, s_ref, *, q, groups, inner, state):
    gw = inner // groups
    hpg = gw // M_HEAD_DIM
    dt = _softplus(dt_ref[...] + dtb_ref[...])
    a_neg = -jnp.exp(alog_ref[...])
    tril = _tril_mask(q)
    acum = _dot_sel_lhs(_as_sel(tril), dt * a_neg)
    acum_t = acum.T
    dt3 = _split3(dt)
    ac3 = _split3(acum)
    lane_head = lax.broadcasted_iota(jnp.int32, (q, gw), 1) // M_HEAD_DIM
    head_sel = [jnp.where(lane_head == hh, 1.0, 0.0).astype(BF16) for hh in range(hpg)]

    for g in range(groups):
        xs = xbc_ref[:, g * gw:(g + 1) * gw].astype(F32)
        bm16 = xbc_ref[:, inner + g * state:inner + (g + 1) * state]
        c_lo = inner + groups * state + g * state
        cm16 = xbc_ref[:, c_lo:c_lo + state]
        e_g = ehead_ref[:, g * gw:(g + 1) * gw]
        dt_x = _select(dt3, e_g)
        ac_x = _select(ac3, e_g)
        xdt = xs * dt_x
        xdt16 = xdt.astype(BF16)
        cb = _dot_nt(cm16, bm16)
        lhs, rhs = [], []
        for hh in range(hpg):
            h = g * hpg + hh
            diff = jnp.broadcast_to(acum[:, h:h + 1], (q, q)) - acum_t[h:h + 1, :]
            dec = jnp.exp(jnp.where(tril, diff, NEG_INF))
            lhs.append((cb * dec).astype(BF16))
            rhs.append(xdt16 * head_sel[hh])
        y = _dot(jnp.concatenate(lhs, axis=1), jnp.concatenate(rhs, axis=0))
        s_old = s_ref[g]
        y = y + _dot(cm16, s_old.astype(BF16)) * jnp.exp(ac_x)
        y = y + dskip_ref[:, g * gw:(g + 1) * gw] * xs
        last = ac_x[q - 1:q, :]
        xw = (xdt * jnp.exp(last - ac_x)).astype(BF16)
        s_ref[g] = s_old * jnp.exp(last) + _dot_tn(bm16, xw)
        y = y * zs_ref[:, g * gw:(g + 1) * gw].astype(F32)
        o_ref[:, g * gw:(g + 1) * gw] = (y * _rms_scale(y) * nw_ref[:, g * gw:(g + 1) * gw]).astype(BF16)


def ssd_mixer_core(act, dt_raw, dt_bias, a_log, d_skip, norm_w, *, bsz, seq, inner, heads):
    t = act.shape[0]
    q = SSD_CHUNK
    rows = q * SSD_CHUNKS_PER_STEP
    assert seq % rows == 0
    nc = seq // rows
    conv_dim = inner + 2 * M_GROUPS * M_STATE
    assert conv_dim == 2 * inner and heads <= LANES_V7X
    pad = LANES_V7X - heads
    head_ids = jnp.arange(LANES_V7X)
    n_split = 3
    ehead = (head_ids[:, None] == (jnp.arange(inner) // M_HEAD_DIM)[None, :]).astype(BF16)
    ehead = jnp.concatenate([ehead] * n_split, axis=0)
    z_blk = conv_dim // inner
    row = lambda b, c: b * nc + c
    const = lambda shape: pl.BlockSpec(shape, lambda b, c: (0, 0))
    return pl.pallas_call(
        functools.partial(_ssd_kernel, q=q, groups=M_GROUPS, inner=inner, state=M_STATE),
        grid=(bsz, nc),
        in_specs=[
            pl.BlockSpec((rows, conv_dim), lambda b, c: (row(b, c), 0)),
            pl.BlockSpec((rows, inner), lambda b, c: (row(b, c), z_blk)),
            pl.BlockSpec((rows, LANES_V7X), lambda b, c: (row(b, c), 0)),
            const((1, LANES_V7X)),
            const((1, LANES_V7X)),
            const((1, inner)),
            const((1, inner)),
            const((n_split * LANES_V7X, inner)),
        ],
        out_specs=pl.BlockSpec((rows, inner), lambda b, c: (row(b, c), 0)),
        out_shape=jax.ShapeDtypeStruct((t, inner), BF16),
        scratch_shapes=[pltpu.VMEM((M_GROUPS, M_STATE, inner // M_GROUPS), F32)],
        compiler_params=_cparams(("parallel", "arbitrary")),
        name="ssd_core",
    )(act, act, dt_raw,
      jnp.pad(dt_bias, (0, pad)).reshape(1, LANES_V7X), jnp.pad(a_log, (0, pad)).reshape(1, LANES_V7X),
      jnp.repeat(d_skip, M_HEAD_DIM).reshape(1, inner), norm_w.reshape(1, inner), ehead)


def _hgrn2_kernel(q_ref, f_ref, i_ref, g_ref, lb_ref, nw_ref, o_ref, s_ref, *, q, heads):
    c = pl.program_id(1)
    dk = H_EXPAND

    @pl.when(c == 0)
    def _():
        s_ref[...] = jnp.zeros_like(s_ref)

    lb = lb_ref[...]
    tril = _tril_mask(q)
    tril_sel = _as_sel(tril)
    for sc in range(q_ref.shape[0] // q):
        rows = slice(sc * q, (sc + 1) * q)
        forget = lb + (1.0 - lb) * _sigmoid(f_ref[rows, :])
        gc = _dot_sel_lhs(tril_sel, jnp.log(forget))
        key = 1.0 - forget
        qs = _silu(q_ref[rows, :].astype(F32)) * (dk ** -0.5)
        mid = q // 2 - 1
        g_mid = gc[mid:mid + 1, :]
        g_last = gc[q - 1:q, :]
        q_mid = qs * jnp.exp(gc - g_mid)
        k_mid = key * jnp.exp(g_mid - gc)
        q_dec = q_mid.astype(BF16)
        k_inv = k_mid.astype(BF16)
        q_in = (q_mid * jnp.exp(g_mid)).astype(BF16)
        k_end = (k_mid * jnp.exp(g_last - g_mid)).astype(BF16)
        e_last = jnp.exp(g_last)
        for h in range(heads):
            sl = slice(h * dk, (h + 1) * dk)
            v_h = i_ref[rows, sl]
            att = jnp.where(tril, _dot_nt(q_dec[:, sl], k_inv[:, sl]), 0.0)
            st_old = s_ref[h]
            o_h = _dot(att.astype(BF16), v_h) + _dot_nt(q_in[:, sl], st_old.astype(BF16))
            s_ref[h] = st_old * e_last[:, sl] + _dot_tn(v_h, k_end[:, sl])
            o_h = o_h * _rms_scale(o_h) * nw_ref[...]
            o_ref[rows, sl] = (o_h * _silu(g_ref[rows, sl].astype(F32))).astype(BF16)


def hgrn2_mixer_core(qig, f_raw, lower_bound, norm_w, *, bsz, seq, d):
    t = qig.shape[0]
    q = HGRN_CHUNK
    rows = q * HGRN_CHUNKS_PER_STEP
    assert seq % rows == 0
    nc = seq // rows
    heads = d // H_EXPAND
    row = lambda b, c: b * nc + c
    part = lambda k: pl.BlockSpec((rows, d), lambda b, c: (row(b, c), k))
    return pl.pallas_call(
        functools.partial(_hgrn2_kernel, q=q, heads=heads),
        grid=(bsz, nc),
        in_specs=[part(0), part(0), part(1), part(2),
                  pl.BlockSpec((1, d), lambda b, c: (0, 0)),
                  pl.BlockSpec((1, H_EXPAND), lambda b, c: (0, 0))],
        out_specs=pl.BlockSpec((rows, d), lambda b, c: (row(b, c), 0)),
        out_shape=jax.ShapeDtypeStruct((t, d), BF16),
        scratch_shapes=[pltpu.VMEM((heads, H_EXPAND, d // heads), F32)],
        compiler_params=_cparams(("parallel", "arbitrary")),
        name="hgrn2_core",
    )(qig, f_raw, qig, qig, lower_bound.reshape(1, d), norm_w.reshape(1, H_EXPAND))


def _block_diag(blocks):
    n = len(blocks)
    rows = []
    for e, blk in enumerate(blocks):
        zero = jnp.zeros_like(blk)
        rows.append(jnp.concatenate([blk if k == e else zero for k in range(n)], axis=1))
    return jnp.concatenate(rows, axis=0)


def _gdn_kernel(qkv_ref, zs_ref, ba_ref, dtb_ref, alog_ref, nw_ref, o_ref, s_ref, *, q, qk_heads, v_heads):
    @pl.when(pl.program_id(1) == 0)
    def _():
        s_ref[...] = jnp.zeros_like(s_ref)

    for sc in range(qkv_ref.shape[0] // q):
        rows = pl.ds(sc * q, q)
        _gdn_chunk(qkv_ref.at[rows], zs_ref.at[rows], ba_ref.at[rows], dtb_ref, alog_ref, nw_ref,
                   o_ref.at[rows], s_ref, q=q, qk_heads=qk_heads, v_heads=v_heads)


def _gdn_chunk(qkv_ref, zs_ref, ba_ref, dtb_ref, alog_ref, nw_ref, o_ref, s_ref, *, q, qk_heads, v_heads):
    dh = G_HEAD_DIM
    key_dim = qk_heads * dh
    rep = v_heads // qk_heads
    assert rep == 2 and GDN_QUAD % rep == 0 and 2 * q == LANES_V7X
    tril = _tril_mask(q)
    beta = _sigmoid(ba_ref[:, 0:LANES_V7X])
    gate = -jnp.exp(alog_ref[...]) * _softplus(ba_ref[:, LANES_V7X:2 * LANES_V7X] + dtb_ref[...])
    gc = _dot_sel_lhs(_as_sel(tril), gate)

    eye_t = (lax.broadcasted_iota(jnp.int32, (q, v_heads * q), 0)
             == lax.broadcasted_iota(jnp.int32, (q, v_heads * q), 1) % q)
    col_s = lax.broadcasted_iota(jnp.int32, (q, v_heads * q), 1) % q
    row_l = lax.broadcasted_iota(jnp.int32, (q, v_heads * q), 0)
    lane_lo = lax.broadcasted_iota(jnp.int32, (q, 2 * q), 1) < q
    g_wide = [jnp.broadcast_to(gc[:, h:h + 1], (q, dh)) for h in range(v_heads)]
    b_wide = [jnp.broadcast_to(beta[:, h:h + 1], (q, dh)) for h in range(v_heads)]
    g_col = jnp.concatenate([jnp.where(lane_lo, g_wide[h], g_wide[h + 1]) for h in range(0, v_heads, 2)], axis=1)
    b_col = jnp.concatenate([jnp.where(lane_lo, b_wide[h], b_wide[h + 1]) for h in range(0, v_heads, 2)], axis=1)
    g_row = jnp.sum(jnp.where(eye_t, g_col, 0.0), axis=0, keepdims=True)
    decay = jnp.exp(jnp.where(row_l >= col_s, g_col - g_row, NEG_INF))

    kk_parts, qk_parts, k_heads, k_f32, q_heads = [], [], [], [], []
    for p in range(qk_heads):
        q_p = qkv_ref[:, p * dh:(p + 1) * dh].astype(F32)
        k_p = qkv_ref[:, key_dim + p * dh:key_dim + (p + 1) * dh].astype(F32)
        q_p = q_p * lax.rsqrt(jnp.sum(q_p * q_p, axis=-1, keepdims=True) + EPS) * (dh ** -0.5)
        k_p = k_p * lax.rsqrt(jnp.sum(k_p * k_p, axis=-1, keepdims=True) + EPS)
        k16 = k_p.astype(BF16)
        q16 = q_p.astype(BF16)
        k_rep = jnp.concatenate([k16] * rep, axis=0)
        kk_parts.append(_dot_nt(k16, k_rep))
        qk_parts.append(_dot_nt(q16, k_rep))
        k_heads.append(k16)
        k_f32.append(k_p)
        q_heads.append(q16)
    kk = jnp.concatenate(kk_parts, axis=1)
    qk = jnp.concatenate(qk_parts, axis=1)
    m_all = jnp.where(row_l > col_s, b_col * kk * decay, 0.0)
    att_all = qk * decay

    quad_w = GDN_QUAD * q
    n_quads = v_heads // GDN_QUAD
    lane_blk = lax.broadcasted_iota(jnp.int32, (q, quad_w), 1) // q
    blk_sel = [jnp.where(lane_blk == e, 1.0, 0.0).astype(BF16) for e in range(GDN_QUAD)]

    def blocks_on_diagonal(x16):
        return jnp.concatenate([x16 * blk_sel[e] for e in range(GDN_QUAD)], axis=0)

    def times_blockwise(lhs_hi, lhs_lo, bd_hi, bd_lo):
        lhs = jnp.concatenate([jnp.concatenate([hi, lo, hi], axis=1) for hi, lo in zip(lhs_hi, lhs_lo)], axis=0)
        out = _dot(lhs, jnp.concatenate([bd_hi, bd_hi, bd_lo], axis=0))
        return [out[i * q:(i + 1) * q, :] for i in range(len(lhs_hi))]

    levels = q.bit_length() - 2
    p_cur = [-m_all[:, u * quad_w:(u + 1) * quad_w] for u in range(n_quads)]
    n_acc = list(p_cur)
    for lvl in range(levels + 1):
        for u in range(n_quads):
            p_hi, p_lo = _split2(p_cur[u])
            bd_hi, bd_lo = blocks_on_diagonal(p_hi), blocks_on_diagonal(p_lo)
            if lvl == 0:
                (p_cur[u],) = times_blockwise([p_hi], [p_lo], bd_hi, bd_lo)
            elif lvl < levels:
                n_hi, n_lo = _split2(n_acc[u])
                n_p, p_sq = times_blockwise([n_hi, p_hi], [n_lo, p_lo], bd_hi, bd_lo)
                n_acc[u] = n_acc[u] + p_cur[u] + n_p
                p_cur[u] = p_sq
            else:
                n_hi, n_lo = _split2(n_acc[u])
                (n_p,) = times_blockwise([n_hi], [n_lo], bd_hi, bd_lo)
                n_acc[u] = n_acc[u] + p_cur[u] + n_p

    def two_pass(a, rhs16):
        a_hi, a_lo = _split2(a)
        return _dot(jnp.concatenate([a_hi, a_lo], axis=1), jnp.concatenate([rhs16, rhs16], axis=0))

    b_row = jnp.sum(jnp.where(eye_t, b_col, 0.0), axis=0, keepdims=True)
    bg_row = b_row * jnp.exp(g_row)
    eye_f = jnp.where(eye_t, 1.0, 0.0)
    zero_blk = jnp.zeros((dh, dh), BF16)
    pairs_per_quad = GDN_QUAD // rep
    for u in range(n_quads):
        cols = slice(u * quad_w, (u + 1) * quad_w)
        t_b = (n_acc[u] + eye_f[:, cols]) * b_row[:, cols]
        t_bg = (n_acc[u] + eye_f[:, cols]) * bg_row[:, cols]
        v16 = [qkv_ref[:, 2 * key_dim + h * dh:2 * key_dim + (h + 1) * dh]
               for h in range(u * GDN_QUAD, (u + 1) * GDN_QUAD)]
        k16 = [k_heads[h // rep] for h in range(u * GDN_QUAD, (u + 1) * GDN_QUAD)]
        u_quad = two_pass(t_b, _block_diag(v16))
        w_quad = two_pass(t_bg, _block_diag(k16))
        v_new, o_inter = [], []
        for pi in range(pairs_per_quad):
            p = u * pairs_per_quad + pi
            pc = slice(pi * rep * dh, (pi + 1) * rep * dh)
            g_pair = jnp.concatenate(g_wide[p * rep:(p + 1) * rep], axis=1)
            g_last = g_pair[q - 1:q, :]
            s_old = s_ref[p]
            s16 = s_old.astype(BF16)
            s_bd = jnp.concatenate([jnp.concatenate([s16[:, 0:dh], zero_blk], axis=1),
                                    jnp.concatenate([zero_blk, s16[:, dh:2 * dh]], axis=1)], axis=0)
            lhs = jnp.concatenate([w_quad[:, pc].astype(BF16), jnp.concatenate([q_heads[p]] * rep, axis=1)],
                                  axis=0)
            both = _dot(lhs, s_bd)
            vn = u_quad[:, pc] - both[0:q, :]
            o_inter.append(both[q:2 * q, :] * jnp.exp(g_pair))
            vn16 = vn.astype(BF16)
            zero_q = jnp.zeros((q, dh), BF16)
            vn_bd = jnp.concatenate([jnp.concatenate([vn16[:, 0:dh], zero_q], axis=1),
                                     jnp.concatenate([zero_q, vn16[:, dh:2 * dh]], axis=1)], axis=0)
            k_dec = jnp.concatenate([k_f32[p]] * rep, axis=1) * jnp.exp(g_last - g_pair)
            ke = jnp.concatenate([k_dec[:, 0:dh], k_dec[:, dh:2 * dh]], axis=0).astype(BF16)
            s_ref[p] = s_old * jnp.exp(g_last) + _dot_tn(ke, vn_bd)
            v_new.extend([vn16[:, 0:dh], vn16[:, dh:2 * dh]])
        o_intra = _dot(att_all[:, cols].astype(BF16), _block_diag(v_new))
        o_quad = jnp.concatenate(o_inter, axis=1) + o_intra
        for e in range(GDN_QUAD):
            h = u * GDN_QUAD + e
            sl = slice(h * dh, (h + 1) * dh)
            o_h = o_quad[:, e * dh:(e + 1) * dh]
            o_h = o_h * _rms_scale(o_h) * nw_ref[...]
            o_ref[:, sl] = (o_h * zs_ref[:, sl].astype(F32)).astype(BF16)


def gdn_mixer_core(act, ba_raw, a_log, dt_bias, norm_w, *, bsz, seq, qk_heads, v_heads):
    t = act.shape[0]
    q = GDN_CHUNK
    rows = q * GDN_CHUNKS_PER_STEP
    assert seq % rows == 0
    nc = seq // rows
    dh = G_HEAD_DIM
    key_dim = qk_heads * dh
    val_dim = v_heads * dh
    conv_dim = 2 * key_dim + val_dim
    assert conv_dim == 2 * val_dim and v_heads % GDN_QUAD == 0
    pad = LANES_V7X - v_heads
    z_blk = conv_dim // val_dim
    row = lambda b, c: b * nc + c
    const = lambda shape: pl.BlockSpec(shape, lambda b, c: (0, 0))
    return pl.pallas_call(
        functools.partial(_gdn_kernel, q=q, qk_heads=qk_heads, v_heads=v_heads),
        grid=(bsz, nc),
        in_specs=[
            pl.BlockSpec((rows, conv_dim), lambda b, c: (row(b, c), 0)),
            pl.BlockSpec((rows, val_dim), lambda b, c: (row(b, c), z_blk)),
            pl.BlockSpec((rows, 2 * LANES_V7X), lambda b, c: (row(b, c), 0)),
            const((1, LANES_V7X)),
            const((1, LANES_V7X)),
            const((1, dh)),
        ],
        out_specs=pl.BlockSpec((rows, val_dim), lambda b, c: (row(b, c), 0)),
        out_shape=jax.ShapeDtypeStruct((t, val_dim), BF16),
        scratch_shapes=[pltpu.VMEM((qk_heads, dh, (v_heads // qk_heads) * dh), F32)],
        compiler_params=_cparams(("parallel", "arbitrary")),
        name="gdn_core",
    )(act, act, ba_raw, jnp.pad(dt_bias, (0, pad)).reshape(1, LANES_V7X),
      jnp.pad(a_log, (0, pad)).reshape(1, LANES_V7X), norm_w.reshape(1, dh))


def _pad_cols(w, width):
    return jnp.pad(w, ((0, 0), (0, width - w.shape[1])))


def _ssd_in_weights(in_w, inner, conv_dim):
    z = in_w[:, :inner]
    xbc = in_w[:, inner:inner + conv_dim]
    dt = _pad_cols(in_w[:, inner + conv_dim:], LANES_V7X)
    return jnp.concatenate([xbc, z], axis=1).astype(BF16), dt.astype(BF16)


def _gdn_in_weights(in_w, conv_dim, val_dim, v_heads):
    main = in_w[:, :conv_dim + val_dim]
    b = _pad_cols(in_w[:, conv_dim + val_dim:conv_dim + val_dim + v_heads], LANES_V7X)
    a = _pad_cols(in_w[:, conv_dim + val_dim + v_heads:], LANES_V7X)
    return main.astype(BF16), jnp.concatenate([b, a], axis=1).astype(BF16)


def _pick_tile(n, candidates):
    for cand in candidates:
        if n % cand == 0:
            return cand
    raise ValueError(f"no tile for {n} in {candidates}")


def kernel(x, mem, ln_mix, ln_xattn, ln_mem, ln_ffn, final_norm, m_in_w, m_conv_w, m_conv_b, m_dt_bias, m_a_log, m_d, m_norm_w, m_out_w, h_in_w, h_lower_bounds, h_norm_w, h_out_w, g_in_w, g_conv_w, g_a_log, g_dt_bias, g_norm_w, g_out_w, xa_q, xa_kv, xa_o, f_up, f_conv_w, f_conv_b, f_down):
    bsz, seq, d = x.shape
    n_mem = mem.shape[1]
    depth = ln_mix.shape[0]
    t = bsz * seq
    tm = _pick_tile(t, (1024, 512, 256, 128, 64))
    tq = _pick_tile(seq, (512, 256, 128, 64))
    ts = _pick_tile(seq, (1024, 512, 256, 128, 64))

    lb = jnp.cumsum(jax.nn.softmax(h_lower_bounds.astype(F32), axis=0), axis=0)
    lb = lb - lb[:1]

    m_inner = m_out_w.shape[1]
    m_heads = m_dt_bias.shape[1]
    m_conv_dim = m_conv_w.shape[2]
    g_v_heads = g_a_log.shape[1]
    g_val_dim = g_out_w.shape[1]
    g_conv_dim = g_conv_w.shape[2]
    g_qk_heads = (g_conv_dim - g_val_dim) // (2 * G_HEAD_DIM)

    xa_q16, xa_kv16, xa_o16 = (w.astype(BF16) for w in (xa_q, xa_kv, xa_o))
    h_in16 = jnp.concatenate([h_in_w[..., :d], h_in_w[..., 2 * d:], h_in_w[..., d:2 * d]], axis=-1).astype(BF16)
    f_up16, f_down16 = f_up.astype(BF16), f_down.astype(BF16)
    m_out16, h_out16, g_out16 = m_out_w.astype(BF16), h_out_w.astype(BF16), g_out_w.astype(BF16)

    xf = x.reshape(t, d)
    memf = mem.reshape(bsz * n_mem, d)
    ia = ib = ic = 0
    for i in range(depth):
        kind = i % N_MIXERS
        if kind == 0:
            w_in, w_dt = _ssd_in_weights(m_in_w[ia], m_inner, m_conv_dim)
            act, dt_raw = mixer_in_proj(xf, ln_mix[i], w_in, w_dt, m_conv_w[ia], m_conv_b[ia],
                                        bsz=bsz, seq=seq, tm=ts, tn=1024)
            y = ssd_mixer_core(act, dt_raw, m_dt_bias[ia], m_a_log[ia], m_d[ia],
                               m_norm_w[ia], bsz=bsz, seq=seq, inner=m_inner, heads=m_heads)
            xf = matmul_residual(y, m_out16, xf, tm=ts, layer=ia)
            ia += 1
        elif kind == 1:
            qig, f_raw = norm_matmul_split(xf, ln_mix[i], h_in16, n16=3 * d, layer=ib, tm=tm, tn=1024)
            y = hgrn2_mixer_core(qig, f_raw, lb[i], h_norm_w[ib], bsz=bsz, seq=seq, d=d)
            xf = matmul_residual(y, h_out16, xf, tm=ts, layer=ib)
            ib += 1
        else:
            w_in, w_ba = _gdn_in_weights(g_in_w[ic], g_conv_dim, g_val_dim, g_v_heads)
            act, ba_raw = mixer_in_proj(xf, ln_mix[i], w_in, w_ba, g_conv_w[ic], jnp.zeros((g_conv_dim,), F32),
                                        bsz=bsz, seq=seq, tm=ts, tn=1024)
            y = gdn_mixer_core(act, ba_raw, g_a_log[ic], g_dt_bias[ic], g_norm_w[ic],
                               bsz=bsz, seq=seq, qk_heads=g_qk_heads, v_heads=g_v_heads)
            xf = matmul_residual(y, g_out16, xf, tm=ts, layer=ic)
            ic += 1

        kv = norm_matmul(memf, ln_mem[i], xa_kv16, layer=i, tm=_pick_tile(bsz * n_mem, (1024, 512, 256)),
                         tn=1024, out_dtype=BF16)
        xf = memory_cross_attention(xf, ln_xattn[i], xa_q16, kv, xa_o16, layer=i,
                                    bsz=bsz, seq=seq, n_mem=n_mem, tq=ts)

        last = i == depth - 1
        xf = conv_glu_ffn(xf, ln_ffn[i], f_up16, f_conv_w[i], f_conv_b[i], f_down16, layer=i,
                          bsz=bsz, seq=seq, tm=tq, final_gain=final_norm if last else None)
    return xf.reshape(bsz, seq, d)
```

```python
import functools

import jax
import jax.numpy as jnp
from jax import lax
from jax.experimental import pallas as pl
from jax.experimental.pallas import tpu as pltpu

F32 = jnp.float32
BF16 = jnp.bfloat16
EPS = 1e-6
NEG_INF = float("-inf")

LANES_V7X = 128
SUBLANES_V7X = 8
VMEM_LIMIT_BYTES_V7X = 56 * 1024 * 1024

CONV_W = 4
FFN_CONV_W = 3
HALO = SUBLANES_V7X

M_HEAD_DIM = 64
M_GROUPS = 8
M_STATE = 128
SSD_CHUNK = 128
SSD_CHUNKS_PER_STEP = 8
H_EXPAND = 128
HGRN_CHUNK = 64
HGRN_CHUNKS_PER_STEP = 8
G_HEAD_DIM = 128
GDN_CHUNK = 64
GDN_CHUNKS_PER_STEP = 8
GDN_QUAD = 4
X_HEADS = 4
N_MIXERS = 3


def _cparams(sem):
    return pltpu.CompilerParams(dimension_semantics=sem, vmem_limit_bytes=VMEM_LIMIT_BYTES_V7X)


def _dot(a, b):
    return jnp.dot(a, b, preferred_element_type=F32)


def _dot_nt(a, b):
    return lax.dot_general(a, b, (((1,), (1,)), ((), ())), preferred_element_type=F32)


def _dot_tn(a, b):
    return lax.dot_general(a, b, (((0,), (0,)), ((), ())), preferred_element_type=F32)


def _split3(x):
    hi = x.astype(BF16)
    r1 = x - hi.astype(F32)
    mid = r1.astype(BF16)
    lo = (r1 - mid.astype(F32)).astype(BF16)
    return hi, mid, lo


def _dot_sel_lhs(sel, x):
    parts = _split3(x)
    return _dot(jnp.concatenate([sel] * len(parts), axis=1), jnp.concatenate(parts, axis=0))


def _sigmoid(x):
    return 0.5 + 0.5 * jnp.tanh(0.5 * x)


def _silu(x):
    half = 0.5 * x
    return half + half * jnp.tanh(half)


def _softplus(x):
    return jnp.maximum(x, 0.0) + jnp.log1p(jnp.exp(-jnp.abs(x)))


def _rms_scale(x):
    return lax.rsqrt(jnp.mean(x * x, axis=-1, keepdims=True) + EPS)


def _tril_mask(n):
    row = lax.broadcasted_iota(jnp.int32, (n, n), 0)
    col = lax.broadcasted_iota(jnp.int32, (n, n), 1)
    return row >= col


def _as_sel(mask):
    return jnp.where(mask, 1.0, 0.0).astype(BF16)


def _shift_rows_down(x, history, s):
    rows, n = x.shape
    groups = x.reshape(rows // SUBLANES_V7X, SUBLANES_V7X, n)
    rolled = pltpu.roll(groups, s, axis=1)
    hist_rolled = pltpu.roll(history, s, axis=0).reshape(1, SUBLANES_V7X, n)
    prev = jnp.concatenate([hist_rolled, rolled[:-1]], axis=0)
    sub = lax.broadcasted_iota(jnp.int32, rolled.shape, 1)
    return jnp.where(sub < s, prev, rolled).reshape(rows, n)


def _norm_matmul_kernel(x_ref, g_ref, w_ref, o_ref, xn_ref):
    @pl.when(pl.program_id(1) == 0)
    def _():
        x = x_ref[...]
        xn_ref[...] = (x * _rms_scale(x) * g_ref[...]).astype(BF16)

    o_ref[...] = _dot(xn_ref[...], w_ref[...]).astype(o_ref.dtype)


def norm_matmul(x, gain, w, *, tm, tn, layer=0, out_dtype=F32):
    t, d = x.shape
    n = w.shape[2]
    assert t % tm == 0 and n % tn == 0, (t, tm, n, tn)
    return pl.pallas_call(
        _norm_matmul_kernel,
        grid=(t // tm, n // tn),
        in_specs=[
            pl.BlockSpec((tm, d), lambda i, j: (i, 0)),
            pl.BlockSpec((1, d), lambda i, j: (0, 0)),
            pl.BlockSpec((None, d, tn), lambda i, j: (layer, 0, j)),
        ],
        out_specs=pl.BlockSpec((tm, tn), lambda i, j: (i, j)),
        out_shape=jax.ShapeDtypeStruct((t, n), out_dtype),
        scratch_shapes=[pltpu.VMEM((tm, d), BF16)],
        compiler_params=_cparams(("parallel", "arbitrary")),
        name="norm_matmul",
    )(x, gain.reshape(1, d), w)


def _norm_matmul_split_kernel(x_ref, g_ref, w_ref, o16_ref, o32_ref, xn_ref, *, n16_tiles):
    j = pl.program_id(1)

    @pl.when(j == 0)
    def _():
        x = x_ref[...]
        xn_ref[...] = (x * _rms_scale(x) * g_ref[...]).astype(BF16)

    @pl.when(j < n16_tiles)
    def _():
        o16_ref[...] = _dot(xn_ref[...], w_ref[...]).astype(BF16)

    @pl.when(j >= n16_tiles)
    def _():
        o32_ref[...] = _dot(xn_ref[...], w_ref[...])


def norm_matmul_split(x, gain, w, *, n16, tm, tn, layer=0):
    t, d = x.shape
    n = w.shape[2]
    assert t % tm == 0 and n16 % tn == 0 and (n - n16) % tn == 0
    n16_tiles = n16 // tn
    return pl.pallas_call(
        functools.partial(_norm_matmul_split_kernel, n16_tiles=n16_tiles),
        grid=(t // tm, n // tn),
        in_specs=[
            pl.BlockSpec((tm, d), lambda i, j: (i, 0)),
            pl.BlockSpec((1, d), lambda i, j: (0, 0)),
            pl.BlockSpec((None, d, tn), lambda i, j: (layer, 0, j)),
        ],
        out_specs=[
            pl.BlockSpec((tm, tn), lambda i, j: (i, jnp.minimum(j, n16_tiles - 1))),
            pl.BlockSpec((tm, tn), lambda i, j: (i, jnp.maximum(j - n16_tiles, 0))),
        ],
        out_shape=[jax.ShapeDtypeStruct((t, n16), BF16), jax.ShapeDtypeStruct((t, n - n16), F32)],
        scratch_shapes=[pltpu.VMEM((tm, d), BF16)],
        compiler_params=_cparams(("parallel", "arbitrary")),
        name="norm_matmul_split",
    )(x, gain.reshape(1, d), w)


def _matmul_res_kernel(a_ref, w_ref, r_ref, o_ref):
    o_ref[...] = r_ref[...] + _dot(a_ref[...], w_ref[...])


def matmul_residual(a, w, res, *, tm, layer=0):
    t, k = a.shape
    d = w.shape[2]
    assert t % tm == 0
    return pl.pallas_call(
        _matmul_res_kernel,
        grid=(t // tm,),
        in_specs=[
            pl.BlockSpec((tm, k), lambda i: (i, 0)),
            pl.BlockSpec((None, k, d), lambda i: (layer, 0, 0)),
            pl.BlockSpec((tm, d), lambda i: (i, 0)),
        ],
        out_specs=pl.BlockSpec((tm, d), lambda i: (i, 0)),
        out_shape=jax.ShapeDtypeStruct((t, d), F32),
        compiler_params=_cparams(("parallel",)),
        name="matmul_residual",
    )(a, w, res)


def _xattn_kernel(x_ref, g_ref, wq_ref, kv_ref, wo_ref, o_ref, *, heads):
    x = x_ref[...]
    d = x.shape[1]
    dh = d // heads
    xn = (x * _rms_scale(x) * g_ref[...]).astype(BF16)
    q = (_dot(xn, wq_ref[...]) * (dh ** -0.5)).astype(BF16)
    outs = []
    for h in range(heads):
        k_h = kv_ref[:, h * dh:(h + 1) * dh]
        v_h = kv_ref[:, d + h * dh:d + (h + 1) * dh]
        s = _dot_nt(q[:, h * dh:(h + 1) * dh], k_h)
        p = jnp.exp(s - jnp.max(s, axis=-1, keepdims=True))
        o_h = _dot(p.astype(BF16), v_h) / jnp.sum(p, axis=-1, keepdims=True)
        outs.append(o_h.astype(BF16))
    o_ref[...] = x + _dot(jnp.concatenate(outs, axis=1), wo_ref[...])


def memory_cross_attention(x, gain, wq, kv, wo, *, layer, bsz, seq, n_mem, tq):
    t, d = x.shape
    nq = seq // tq
    return pl.pallas_call(
        functools.partial(_xattn_kernel, heads=X_HEADS),
        grid=(bsz, nq),
        in_specs=[
            pl.BlockSpec((tq, d), lambda b, i: (b * nq + i, 0)),
            pl.BlockSpec((1, d), lambda b, i: (0, 0)),
            pl.BlockSpec((None, d, d), lambda b, i: (layer, 0, 0)),
            pl.BlockSpec((n_mem, 2 * d), lambda b, i: (b, 0)),
            pl.BlockSpec((None, d, d), lambda b, i: (layer, 0, 0)),
        ],
        out_specs=pl.BlockSpec((tq, d), lambda b, i: (b * nq + i, 0)),
        out_shape=jax.ShapeDtypeStruct((t, d), F32),
        compiler_params=_cparams(("parallel", "parallel")),
        name="memory_xattn",
    )(x, gain.reshape(1, d), wq, kv, wo)


def _ffn_kernel(x_ref, g_ref, wg_ref, wu_ref, cw_ref, cb_ref, wd_ref, *rest, final_norm):
    if final_norm:
        fg_ref, o_ref, act_ref, halo_ref = rest
    else:
        o_ref, act_ref, halo_ref = rest
    tm = x_ref.shape[0]
    f = wg_ref.shape[1]

    @pl.when(pl.program_id(1) == 0)
    def _():
        halo_ref[...] = jnp.zeros_like(halo_ref)

    x = x_ref[...]
    xn = (x * _rms_scale(x) * g_ref[...]).astype(BF16)
    cols_per = 2 * LANES_V7X
    for c in range(0, f, cols_per):
        cols = slice(c, c + cols_per)
        gate = _dot(xn, wg_ref[:, cols])
        up = _dot(xn, wu_ref[:, cols])
        history = halo_ref[:, cols]
        halo_ref[:, cols] = gate[tm - HALO:tm, :]
        conv = cb_ref[:, cols] + cw_ref[FFN_CONV_W - 1:FFN_CONV_W, cols] * gate
        for tap in range(FFN_CONV_W - 1):
            conv = conv + cw_ref[tap:tap + 1, cols] * _shift_rows_down(gate, history, FFN_CONV_W - 1 - tap)
        act_ref[:, cols] = (_silu(conv) * up).astype(BF16)
    y = x + _dot(act_ref[...], wd_ref[...])
    if final_norm:
        y = y * _rms_scale(y) * fg_ref[...]
    o_ref[...] = y


def conv_glu_ffn(x, gain, w_up, conv_w, conv_b, wd, *, layer, bsz, seq, tm, final_gain=None):
    t, d = x.shape
    f = wd.shape[1]
    assert seq % tm == 0 and f % (2 * LANES_V7X) == 0
    ni = seq // tm
    resident = lambda shape: pl.BlockSpec(shape, lambda b, i: (0, 0), pipeline_mode=pl.Buffered(1))
    stacked = lambda shape, col: pl.BlockSpec((None,) + shape, lambda b, i: (layer, 0, col),
                                              pipeline_mode=pl.Buffered(1))
    in_specs = [
        pl.BlockSpec((tm, d), lambda b, i: (b * ni + i, 0)),
        resident((1, d)),
        stacked((d, f), 0),
        stacked((d, f), 1),
        resident((FFN_CONV_W, f)),
        resident((1, f)),
        stacked((f, d), 0),
    ]
    args = [x, gain.reshape(1, d), w_up, w_up, conv_w, conv_b.reshape(1, f), wd]
    if final_gain is not None:
        in_specs.append(resident((1, d)))
        args.append(final_gain.reshape(1, d))
    return pl.pallas_call(
        functools.partial(_ffn_kernel, final_norm=final_gain is not None),
        grid=(bsz, ni),
        in_specs=in_specs,
        out_specs=pl.BlockSpec((tm, d), lambda b, i: (b * ni + i, 0)),
        out_shape=jax.ShapeDtypeStruct((t, d), F32),
        scratch_shapes=[pltpu.VMEM((tm, f), BF16), pltpu.VMEM((HALO, f), F32)],
        compiler_params=_cparams(("parallel", "arbitrary")),
        name="conv_glu_ffn",
    )(*args)


def _inproj_kernel(x_ref, g_ref, w_ref, cw_ref, cb_ref, ws_ref, o_ref, os_ref, xn_ref, gp_ref, halo_ref,
                   *, n_conv_tiles):
    i = pl.program_id(1)
    j = pl.program_id(2)
    tm = x_ref.shape[0]

    @pl.when(j == 0)
    def _():
        x = x_ref[...]
        xn = (x * _rms_scale(x) * g_ref[...]).astype(BF16)
        xn_ref[...] = xn
        os_ref[...] = _dot(xn, ws_ref[...])

    @pl.when(i == 0)
    def _():
        gp_ref[...] = jnp.zeros_like(gp_ref)

    @pl.when(jnp.logical_and(i > 0, j < n_conv_tiles))
    def _():
        gp_ref[...] = halo_ref[j]

    @pl.when(j < n_conv_tiles)
    def _():
        acc = _dot(xn_ref[...], w_ref[...])
        halo_ref[j] = acc[tm - HALO:tm, :]
        conv = cb_ref[...] + cw_ref[CONV_W - 1:CONV_W, :] * acc
        for tap in range(CONV_W - 1):
            conv = conv + cw_ref[tap:tap + 1, :] * _shift_rows_down(acc, gp_ref[...], CONV_W - 1 - tap)
        o_ref[...] = _silu(conv).astype(BF16)

    @pl.when(j >= n_conv_tiles)
    def _():
        o_ref[...] = _silu(_dot(xn_ref[...], w_ref[...])).astype(BF16)


def mixer_in_proj(x, gain, w, w_small, conv_w, conv_b, *, bsz, seq, tm, tn):
    t, d = x.shape
    n = w.shape[1]
    conv_dim = conv_w.shape[1]
    ns = w_small.shape[1]
    assert seq % tm == 0 and n % tn == 0 and conv_dim % tn == 0
    ni, nj, n_conv_tiles = seq // tm, n // tn, conv_dim // tn
    conv_col = lambda b, i, j: (0, jnp.minimum(j, n_conv_tiles - 1))
    return pl.pallas_call(
        functools.partial(_inproj_kernel, n_conv_tiles=n_conv_tiles),
        grid=(bsz, ni, nj),
        in_specs=[
            pl.BlockSpec((tm, d), lambda b, i, j: (b * ni + i, 0)),
            pl.BlockSpec((1, d), lambda b, i, j: (0, 0)),
            pl.BlockSpec((d, tn), lambda b, i, j: (0, j)),
            pl.BlockSpec((CONV_W, tn), conv_col),
            pl.BlockSpec((1, tn), conv_col),
            pl.BlockSpec((d, ns), lambda b, i, j: (0, 0)),
        ],
        out_specs=[
            pl.BlockSpec((tm, tn), lambda b, i, j: (b * ni + i, j)),
            pl.BlockSpec((tm, ns), lambda b, i, j: (b * ni + i, 0)),
        ],
        out_shape=[jax.ShapeDtypeStruct((t, n), BF16), jax.ShapeDtypeStruct((t, ns), F32)],
        scratch_shapes=[
            pltpu.VMEM((tm, d), BF16),
            pltpu.VMEM((HALO, tn), F32),
            pltpu.VMEM((n_conv_tiles, HALO, tn), F32),
        ],
        compiler_params=_cparams(("parallel", "arbitrary", "arbitrary")),
        name="mixer_in_proj",
    )(x, gain.reshape(1, d), w, conv_w, conv_b.reshape(1, conv_dim), w_small)


def _split2(x):
    hi = x.astype(BF16)
    return hi, (x - hi.astype(F32)).astype(BF16)


def _select(parts, sel_stacked):
    k = parts[0].shape[1]
    return _dot(jnp.concatenate(parts, axis=1), sel_stacked[0:len(parts) * k, :])


def _ssd_kernel(xbc_ref, zs_ref, dt_ref, dtb_ref, alog_ref, dskip_ref, nw_ref,
                ehead_ref, o_ref, s_ref, *, q, groups, inner, state):
    @pl.when(pl.program_id(1) == 0)
    def _():
        s_ref[...] = jnp.zeros_like(s_ref)

    for sc in range(xbc_ref.shape[0] // q):
        rows = pl.ds(sc * q, q)
        _ssd_chunk(xbc_ref.at[rows], zs_ref.at[rows], dt_ref.at[rows], dtb_ref, alog_ref, dskip_ref, nw_ref,
                   ehead_ref, o_ref.at[rows], s_ref, q=q, groups=groups, inner=inner, state=state)


def _ssd_chunk(xbc_ref, zs_ref, dt_ref, dtb_ref, alog_ref, dskip_ref, nw_ref,
               ehead_ref, o_ref, s_ref, *, q, groups, inner, state):
    gw = inner // groups
    hpg = gw // M_HEAD_DIM
    dt = _softplus(dt_ref[...] + dtb_ref[...])
    a_neg = -jnp.exp(alog_ref[...])
    tril = _tril_mask(q)
    acum = _dot_sel_lhs(_as_sel(tril), dt * a_neg)
    acum_t = acum.T
    dt2 = _split2(dt)
    ac3 = _split3(acum)
    lane_head = lax.broadcasted_iota(jnp.int32, (q, gw), 1) // M_HEAD_DIM
    head_sel = [jnp.where(lane_head == hh, 1.0, 0.0).astype(BF16) for hh in range(hpg)]

    for g in range(groups):
        xs = xbc_ref[:, g * gw:(g + 1) * gw].astype(F32)
        bm16 = xbc_ref[:, inner + g * state:inner + (g + 1) * state]
        c_lo = inner + groups * state + g * state
        cm16 = xbc_ref[:, c_lo:c_lo + state]
        e_g = ehead_ref[:, g * gw:(g + 1) * gw]
        dt_x = _select(dt2, e_g)
        ac_x = _select(ac3, e_g)
        xdt = xs * dt_x
        xdt16 = xdt.astype(BF16)
        cb = _dot_nt(cm16, bm16)
        lhs, rhs = [], []
        for hh in range(hpg):
            h = g * hpg + hh
            diff = jnp.broadcast_to(acum[:, h:h + 1], (q, q)) - acum_t[h:h + 1, :]
            dec = jnp.exp(jnp.where(tril, diff, NEG_INF))
            lhs.append((cb * dec).astype(BF16))
            rhs.append(xdt16 * head_sel[hh])
        y = _dot(jnp.concatenate(lhs, axis=1), jnp.concatenate(rhs, axis=0))
        s_old = s_ref[g]
        y = y + _dot(cm16, s_old.astype(BF16)) * jnp.exp(ac_x)
        y = y + dskip_ref[:, g * gw:(g + 1) * gw] * xs
        last = ac_x[q - 1:q, :]
        xw = (xdt * jnp.exp(last - ac_x)).astype(BF16)
        s_ref[g] = s_old * jnp.exp(last) + _dot_tn(bm16, xw)
        y = y * zs_ref[:, g * gw:(g + 1) * gw].astype(F32)
        o_ref[:, g * gw:(g + 1) * gw] = (y * _rms_scale(y) * nw_ref[:, g * gw:(g + 1) * gw]).astype(BF16)


def ssd_mixer_core(act, dt_raw, dt_bias, a_log, d_skip, norm_w, *, bsz, seq, inner, heads):
    t = act.shape[0]
    q = SSD_CHUNK
    rows = q * SSD_CHUNKS_PER_STEP
    assert seq % rows == 0
    nc = seq // rows
    conv_dim = inner + 2 * M_GROUPS * M_STATE
    assert conv_dim == 2 * inner and heads <= LANES_V7X
    pad = LANES_V7X - heads
    head_ids = jnp.arange(LANES_V7X)
    n_split = 3
    ehead = (head_ids[:, None] == (jnp.arange(inner) // M_HEAD_DIM)[None, :]).astype(BF16)
    ehead = jnp.concatenate([ehead] * n_split, axis=0)
    z_blk = conv_dim // inner
    row = lambda b, c: b * nc + c
    const = lambda shape: pl.BlockSpec(shape, lambda b, c: (0, 0))
    return pl.pallas_call(
        functools.partial(_ssd_kernel, q=q, groups=M_GROUPS, inner=inner, state=M_STATE),
        grid=(bsz, nc),
        in_specs=[
            pl.BlockSpec((rows, conv_dim), lambda b, c: (row(b, c), 0)),
            pl.BlockSpec((rows, inner), lambda b, c: (row(b, c), z_blk)),
            pl.BlockSpec((rows, LANES_V7X), lambda b, c: (row(b, c), 0)),
            const((1, LANES_V7X)),
            const((1, LANES_V7X)),
            const((1, inner)),
            const((1, inner)),
            const((n_split * LANES_V7X, inner)),
        ],
        out_specs=pl.BlockSpec((rows, inner), lambda b, c: (row(b, c), 0)),
        out_shape=jax.ShapeDtypeStruct((t, inner), BF16),
        scratch_shapes=[pltpu.VMEM((M_GROUPS, M_STATE, inner // M_GROUPS), F32)],
        compiler_params=_cparams(("parallel", "arbitrary")),
        name="ssd_core",
    )(act, act, dt_raw,
      jnp.pad(dt_bias, (0, pad)).reshape(1, LANES_V7X), jnp.pad(a_log, (0, pad)).reshape(1, LANES_V7X),
      jnp.repeat(d_skip, M_HEAD_DIM).reshape(1, inner), norm_w.reshape(1, inner), ehead)


def _hgrn2_kernel(q_ref, f_ref, i_ref, g_ref, lb_ref, nw_ref, o_ref, s_ref, *, q, heads):
    c = pl.program_id(1)
    dk = H_EXPAND

    @pl.when(c == 0)
    def _():
        s_ref[...] = jnp.zeros_like(s_ref)

    lb = lb_ref[...]
    tril = _tril_mask(q)
    tril_sel = _as_sel(tril)
    for sc in range(q_ref.shape[0] // q):
        rows = slice(sc * q, (sc + 1) * q)
        forget = lb + (1.0 - lb) * _sigmoid(f_ref[rows, :])
        gc = _dot_sel_lhs(tril_sel, jnp.log(forget))
        key = 1.0 - forget
        qs = _silu(q_ref[rows, :].astype(F32)) * (dk ** -0.5)
        mid = q // 2 - 1
        g_mid = gc[mid:mid + 1, :]
        g_last = gc[q - 1:q, :]
        q_mid = qs * jnp.exp(gc - g_mid)
        k_mid = key * jnp.exp(g_mid - gc)
        q_dec = q_mid.astype(BF16)
        k_inv = k_mid.astype(BF16)
        q_in = (q_mid * jnp.exp(g_mid)).astype(BF16)
        k_end = (k_mid * jnp.exp(g_last - g_mid)).astype(BF16)
        e_last = jnp.exp(g_last)
        for h in range(heads):
            sl = slice(h * dk, (h + 1) * dk)
            v_h = i_ref[rows, sl]
            att = jnp.where(tril, _dot_nt(q_dec[:, sl], k_inv[:, sl]), 0.0)
            st_old = s_ref[h]
            o_h = _dot(att.astype(BF16), v_h) + _dot_nt(q_in[:, sl], st_old.astype(BF16))
            s_ref[h] = st_old * e_last[:, sl] + _dot_tn(v_h, k_end[:, sl])
            o_h = o_h * _rms_scale(o_h) * nw_ref[...]
            o_ref[rows, sl] = (o_h * _silu(g_ref[rows, sl].astype(F32))).astype(BF16)


def hgrn2_mixer_core(qig, f_raw, lower_bound, norm_w, *, bsz, seq, d):
    t = qig.shape[0]
    q = HGRN_CHUNK
    rows = q * HGRN_CHUNKS_PER_STEP
    assert seq % rows == 0
    nc = seq // rows
    heads = d // H_EXPAND
    row = lambda b, c: b * nc + c
    part = lambda k: pl.BlockSpec((rows, d), lambda b, c: (row(b, c), k))
    return pl.pallas_call(
        functools.partial(_hgrn2_kernel, q=q, heads=heads),
        grid=(bsz, nc),
        in_specs=[part(0), part(0), part(1), part(2),
                  pl.BlockSpec((1, d), lambda b, c: (0, 0)),
                  pl.BlockSpec((1, H_EXPAND), lambda b, c: (0, 0))],
        out_specs=pl.BlockSpec((rows, d), lambda b, c: (row(b, c), 0)),
        out_shape=jax.ShapeDtypeStruct((t, d), BF16),
        scratch_shapes=[pltpu.VMEM((heads, H_EXPAND, d // heads), F32)],
        compiler_params=_cparams(("parallel", "arbitrary")),
        name="hgrn2_core",
    )(qig, f_raw, qig, qig, lower_bound.reshape(1, d), norm_w.reshape(1, H_EXPAND))


def _block_diag(blocks):
    n = len(blocks)
    rows = []
    for e, blk in enumerate(blocks):
        zero = jnp.zeros_like(blk)
        rows.append(jnp.concatenate([blk if k == e else zero for k in range(n)], axis=1))
    return jnp.concatenate(rows, axis=0)


def _gdn_kernel(qkv_ref, zs_ref, ba_ref, dtb_ref, alog_ref, nw_ref, o_ref, s_ref, *, q, qk_heads, v_heads):
    @pl.when(pl.program_id(1) == 0)
    def _():
        s_ref[...] = jnp.zeros_like(s_ref)

    for sc in range(qkv_ref.shape[0] // q):
        rows = pl.ds(sc * q, q)
        _gdn_chunk(qkv_ref.at[rows], zs_ref.at[rows], ba_ref.at[rows], dtb_ref, alog_ref, nw_ref,
                   o_ref.at[rows], s_ref, q=q, qk_heads=qk_heads, v_heads=v_heads)


def _gdn_chunk(qkv_ref, zs_ref, ba_ref, dtb_ref, alog_ref, nw_ref, o_ref, s_ref, *, q, qk_heads, v_heads):
    dh = G_HEAD_DIM
    key_dim = qk_heads * dh
    rep = v_heads // qk_heads
    assert rep == 2 and GDN_QUAD % rep == 0 and 2 * q == LANES_V7X
    tril = _tril_mask(q)
    beta = _sigmoid(ba_ref[:, 0:LANES_V7X])
    gate = -jnp.exp(alog_ref[...]) * _softplus(ba_ref[:, LANES_V7X:2 * LANES_V7X] + dtb_ref[...])
    gc = _dot_sel_lhs(_as_sel(tril), gate)

    eye_t = (lax.broadcasted_iota(jnp.int32, (q, v_heads * q), 0)
             == lax.broadcasted_iota(jnp.int32, (q, v_heads * q), 1) % q)
    col_s = lax.broadcasted_iota(jnp.int32, (q, v_heads * q), 1) % q
    row_l = lax.broadcasted_iota(jnp.int32, (q, v_heads * q), 0)
    lane_lo = lax.broadcasted_iota(jnp.int32, (q, 2 * q), 1) < q
    g_wide = [jnp.broadcast_to(gc[:, h:h + 1], (q, dh)) for h in range(v_heads)]
    b_wide = [jnp.broadcast_to(beta[:, h:h + 1], (q, dh)) for h in range(v_heads)]
    g_col = jnp.concatenate([jnp.where(lane_lo, g_wide[h], g_wide[h + 1]) for h in range(0, v_heads, 2)], axis=1)
    b_col = jnp.concatenate([jnp.where(lane_lo, b_wide[h], b_wide[h + 1]) for h in range(0, v_heads, 2)], axis=1)
    g_row = jnp.sum(jnp.where(eye_t, g_col, 0.0), axis=0, keepdims=True)
    decay = jnp.exp(jnp.where(row_l >= col_s, g_col - g_row, NEG_INF))

    kk_parts, qk_parts, k_heads, k_f32, q_heads = [], [], [], [], []
    for p in range(qk_heads):
        q_p = qkv_ref[:, p * dh:(p + 1) * dh].astype(F32)
        k_p = qkv_ref[:, key_dim + p * dh:key_dim + (p + 1) * dh].astype(F32)
        q_p = q_p * lax.rsqrt(jnp.sum(q_p * q_p, axis=-1, keepdims=True) + EPS) * (dh ** -0.5)
        k_p = k_p * lax.rsqrt(jnp.sum(k_p * k_p, axis=-1, keepdims=True) + EPS)
        k16 = k_p.astype(BF16)
        q16 = q_p.astype(BF16)
        k_rep = jnp.concatenate([k16] * rep, axis=0)
        kk_parts.append(_dot_nt(k16, k_rep))
        qk_parts.append(_dot_nt(q16, k_rep))
        k_heads.append(k16)
        k_f32.append(k_p)
        q_heads.append(q16)
    kk = jnp.concatenate(kk_parts, axis=1)
    qk = jnp.concatenate(qk_parts, axis=1)
    m_all = jnp.where(row_l > col_s, b_col * kk * decay, 0.0)
    att_all = qk * decay

    quad_w = GDN_QUAD * q
    n_quads = v_heads // GDN_QUAD
    lane_blk = lax.broadcasted_iota(jnp.int32, (q, quad_w), 1) // q
    blk_sel = [jnp.where(lane_blk == e, 1.0, 0.0).astype(BF16) for e in range(GDN_QUAD)]

    def blocks_on_diagonal(x16):
        return jnp.concatenate([x16 * blk_sel[e] for e in range(GDN_QUAD)], axis=0)

    def times_blockwise(lhs_hi, lhs_lo, bd_hi, bd_lo):
        lhs = jnp.concatenate([jnp.concatenate([hi, lo, hi], axis=1) for hi, lo in zip(lhs_hi, lhs_lo)], axis=0)
        out = _dot(lhs, jnp.concatenate([bd_hi, bd_hi, bd_lo], axis=0))
        return [out[i * q:(i + 1) * q, :] for i in range(len(lhs_hi))]

    levels = q.bit_length() - 2
    p_cur = [-m_all[:, u * quad_w:(u + 1) * quad_w] for u in range(n_quads)]
    n_acc = list(p_cur)
    for lvl in range(levels + 1):
        for u in range(n_quads):
            p_hi, p_lo = _split2(p_cur[u])
            bd_hi, bd_lo = blocks_on_diagonal(p_hi), blocks_on_diagonal(p_lo)
            if lvl == 0:
                (p_cur[u],) = times_blockwise([p_hi], [p_lo], bd_hi, bd_lo)
            elif lvl < levels:
                n_hi, n_lo = _split2(n_acc[u])
                n_p, p_sq = times_blockwise([n_hi, p_hi], [n_lo, p_lo], bd_hi, bd_lo)
                n_acc[u] = n_acc[u] + p_cur[u] + n_p
                p_cur[u] = p_sq
            else:
                n_hi, n_lo = _split2(n_acc[u])
                (n_p,) = times_blockwise([n_hi], [n_lo], bd_hi, bd_lo)
                n_acc[u] = n_acc[u] + p_cur[u] + n_p

    def two_pass(a, rhs16):
        a_hi, a_lo = _split2(a)
        return _dot(jnp.concatenate([a_hi, a_lo], axis=1), jnp.concatenate([rhs16, rhs16], axis=0))

    b_row = jnp.sum(jnp.where(eye_t, b_col, 0.0), axis=0, keepdims=True)
    bg_row = b_row * jnp.exp(g_row)
    eye_f = jnp.where(eye_t, 1.0, 0.0)
    zero_blk = jnp.zeros((dh, dh), BF16)
    pairs_per_quad = GDN_QUAD // rep
    for u in range(n_quads):
        cols = slice(u * quad_w, (u + 1) * quad_w)
        t_b = (n_acc[u] + eye_f[:, cols]) * b_row[:, cols]
        t_bg = (n_acc[u] + eye_f[:, cols]) * bg_row[:, cols]
        v16 = [qkv_ref[:, 2 * key_dim + h * dh:2 * key_dim + (h + 1) * dh]
               for h in range(u * GDN_QUAD, (u + 1) * GDN_QUAD)]
        k16 = [k_heads[h // rep] for h in range(u * GDN_QUAD, (u + 1) * GDN_QUAD)]
        u_quad = two_pass(t_b, _block_diag(v16))
        w_quad = two_pass(t_bg, _block_diag(k16))
        v_new, o_inter = [], []
        for pi in range(pairs_per_quad):
            p = u * pairs_per_quad + pi
            pc = slice(pi * rep * dh, (pi + 1) * rep * dh)
            g_pair = jnp.concatenate(g_wide[p * rep:(p + 1) * rep], axis=1)
            g_last = g_pair[q - 1:q, :]
            s_old = s_ref[p]
            s16 = s_old.astype(BF16)
            s_bd = jnp.concatenate([jnp.concatenate([s16[:, 0:dh], zero_blk], axis=1),
                                    jnp.concatenate([zero_blk, s16[:, dh:2 * dh]], axis=1)], axis=0)
            lhs = jnp.concatenate([w_quad[:, pc].astype(BF16), jnp.concatenate([q_heads[p]] * rep, axis=1)],
                                  axis=0)
            both = _dot(lhs, s_bd)
            vn = u_quad[:, pc] - both[0:q, :]
            o_inter.append(both[q:2 * q, :] * jnp.exp(g_pair))
            vn16 = vn.astype(BF16)
            zero_q = jnp.zeros((q, dh), BF16)
            vn_bd = jnp.concatenate([jnp.concatenate([vn16[:, 0:dh], zero_q], axis=1),
                                     jnp.concatenate([zero_q, vn16[:, dh:2 * dh]], axis=1)], axis=0)
            k_dec = jnp.concatenate([k_f32[p]] * rep, axis=1) * jnp.exp(g_last - g_pair)
            ke = jnp.concatenate([k_dec[:, 0:dh], k_dec[:, dh:2 * dh]], axis=0).astype(BF16)
            s_ref[p] = s_old * jnp.exp(g_last) + _dot_tn(ke, vn_bd)
            v_new.extend([vn16[:, 0:dh], vn16[:, dh:2 * dh]])
        o_intra = _dot(att_all[:, cols].astype(BF16), _block_diag(v_new))
        o_quad = jnp.concatenate(o_inter, axis=1) + o_intra
        for e in range(GDN_QUAD):
            h = u * GDN_QUAD + e
            sl = slice(h * dh, (h + 1) * dh)
            o_h = o_quad[:, e * dh:(e + 1) * dh]
            o_h = o_h * _rms_scale(o_h) * nw_ref[...]
            o_ref[:, sl] = (o_h * zs_ref[:, sl].astype(F32)).astype(BF16)


def gdn_mixer_core(act, ba_raw, a_log, dt_bias, norm_w, *, bsz, seq, qk_heads, v_heads):
    t = act.shape[0]
    q = GDN_CHUNK
    rows = q * GDN_CHUNKS_PER_STEP
    assert seq % rows == 0
    nc = seq // rows
    dh = G_HEAD_DIM
    key_dim = qk_heads * dh
    val_dim = v_heads * dh
    conv_dim = 2 * key_dim + val_dim
    assert conv_dim == 2 * val_dim and v_heads % GDN_QUAD == 0
    pad = LANES_V7X - v_heads
    z_blk = conv_dim // val_dim
    row = lambda b, c: b * nc + c
    const = lambda shape: pl.BlockSpec(shape, lambda b, c: (0, 0))
    return pl.pallas_call(
        functools.partial(_gdn_kernel, q=q, qk_heads=qk_heads, v_heads=v_heads),
        grid=(bsz, nc),
        in_specs=[
            pl.BlockSpec((rows, conv_dim), lambda b, c: (row(b, c), 0)),
            pl.BlockSpec((rows, val_dim), lambda b, c: (row(b, c), z_blk)),
            pl.BlockSpec((rows, 2 * LANES_V7X), lambda b, c: (row(b, c), 0)),
            const((1, LANES_V7X)),
            const((1, LANES_V7X)),
            const((1, dh)),
        ],
        out_specs=pl.BlockSpec((rows, val_dim), lambda b, c: (row(b, c), 0)),
        out_shape=jax.ShapeDtypeStruct((t, val_dim), BF16),
        scratch_shapes=[pltpu.VMEM((qk_heads, dh, (v_heads // qk_heads) * dh), F32)],
        compiler_params=_cparams(("parallel", "arbitrary")),
        name="gdn_core",
    )(act, act, ba_raw, jnp.pad(dt_bias, (0, pad)).reshape(1, LANES_V7X),
      jnp.pad(a_log, (0, pad)).reshape(1, LANES_V7X), norm_w.reshape(1, dh))


def _pad_cols(w, width):
    return jnp.pad(w, ((0, 0), (0, width - w.shape[1])))


def _ssd_in_weights(in_w, inner, conv_dim):
    z = in_w[:, :inner]
    xbc = in_w[:, inner:inner + conv_dim]
    dt = _pad_cols(in_w[:, inner + conv_dim:], LANES_V7X)
    return jnp.concatenate([xbc, z], axis=1).astype(BF16), dt.astype(BF16)


def _gdn_in_weights(in_w, conv_dim, val_dim, v_heads):
    main = in_w[:, :conv_dim + val_dim]
    b = _pad_cols(in_w[:, conv_dim + val_dim:conv_dim + val_dim + v_heads], LANES_V7X)
    a = _pad_cols(in_w[:, conv_dim + val_dim + v_heads:], LANES_V7X)
    return main.astype(BF16), jnp.concatenate([b, a], axis=1).astype(BF16)


def _pick_tile(n, candidates):
    for cand in candidates:
        if n % cand == 0:
            return cand
    raise ValueError(f"no tile for {n} in {candidates}")


def kernel(x, mem, ln_mix, ln_xattn, ln_mem, ln_ffn, final_norm, m_in_w, m_conv_w, m_conv_b, m_dt_bias, m_a_log, m_d, m_norm_w, m_out_w, h_in_w, h_lower_bounds, h_norm_w, h_out_w, g_in_w, g_conv_w, g_a_log, g_dt_bias, g_norm_w, g_out_w, xa_q, xa_kv, xa_o, f_up, f_conv_w, f_conv_b, f_down):
    bsz, seq, d = x.shape
    n_mem = mem.shape[1]
    depth = ln_mix.shape[0]
    t = bsz * seq
    tm = _pick_tile(t, (1024, 512, 256, 128, 64))
    tq = _pick_tile(seq, (512, 256, 128, 64))
    ts = _pick_tile(seq, (1024, 512, 256, 128, 64))

    lb = jnp.cumsum(jax.nn.softmax(h_lower_bounds.astype(F32), axis=0), axis=0)
    lb = lb - lb[:1]

    m_inner = m_out_w.shape[1]
    m_heads = m_dt_bias.shape[1]
    m_conv_dim = m_conv_w.shape[2]
    g_v_heads = g_a_log.shape[1]
    g_val_dim = g_out_w.shape[1]
    g_conv_dim = g_conv_w.shape[2]
    g_qk_heads = (g_conv_dim - g_val_dim) // (2 * G_HEAD_DIM)

    xa_q16, xa_kv16, xa_o16 = (w.astype(BF16) for w in (xa_q, xa_kv, xa_o))
    h_in16 = jnp.concatenate([h_in_w[..., :d], h_in_w[..., 2 * d:], h_in_w[..., d:2 * d]], axis=-1).astype(BF16)
    f_up16, f_down16 = f_up.astype(BF16), f_down.astype(BF16)
    m_out16, h_out16, g_out16 = m_out_w.astype(BF16), h_out_w.astype(BF16), g_out_w.astype(BF16)

    xf = x.reshape(t, d)
    memf = mem.reshape(bsz * n_mem, d)
    ia = ib = ic = 0
    for i in range(depth):
        kind = i % N_MIXERS
        if kind == 0:
            w_in, w_dt = _ssd_in_weights(m_in_w[ia], m_inner, m_conv_dim)
            act, dt_raw = mixer_in_proj(xf, ln_mix[i], w_in, w_dt, m_conv_w[ia], m_conv_b[ia],
                                        bsz=bsz, seq=seq, tm=ts, tn=1024)
            y = ssd_mixer_core(act, dt_raw, m_dt_bias[ia], m_a_log[ia], m_d[ia],
                               m_norm_w[ia], bsz=bsz, seq=seq, inner=m_inner, heads=m_heads)
            xf = matmul_residual(y, m_out16, xf, tm=ts, layer=ia)
            ia += 1
        elif kind == 1:
            qig, f_raw = norm_matmul_split(xf, ln_mix[i], h_in16, n16=3 * d, layer=ib, tm=tm, tn=1024)
            y = hgrn2_mixer_core(qig, f_raw, lb[i], h_norm_w[ib], bsz=bsz, seq=seq, d=d)
            xf = matmul_residual(y, h_out16, xf, tm=ts, layer=ib)
            ib += 1
        else:
            w_in, w_ba = _gdn_in_weights(g_in_w[ic], g_conv_dim, g_val_dim, g_v_heads)
            act, ba_raw = mixer_in_proj(xf, ln_mix[i], w_in, w_ba, g_conv_w[ic], jnp.zeros((g_conv_dim,), F32),
                                        bsz=bsz, seq=seq, tm=ts, tn=1024)
            y = gdn_mixer_core(act, ba_raw, g_a_log[ic], g_dt_bias[ic], g_norm_w[ic],
                               bsz=bsz, seq=seq, qk_heads=g_qk_heads, v_heads=g_v_heads)
            xf = matmul_residual(y, g_out16, xf, tm=ts, layer=ic)
            ic += 1

        kv = norm_matmul(memf, ln_mem[i], xa_kv16, layer=i, tm=_pick_tile(bsz * n_mem, (1024, 512, 256)),
                         tn=1024, out_dtype=BF16)
        xf = memory_cross_attention(xf, ln_xattn[i], xa_q16, kv, xa_o16, layer=i,
                                    bsz=bsz, seq=seq, n_mem=n_mem, tq=ts)

        last = i == depth - 1
        xf = conv_glu_ffn(xf, ln_ffn[i], f_up16, f_conv_w[i], f_conv_b[i], f_down16, layer=i,
                          bsz=bsz, seq=seq, tm=tq, final_gain=final_norm if last else None)
    return xf.reshape(bsz, seq, d)
```

```python
import functools

import jax
import jax.numpy as jnp
from jax import lax
from jax.experimental import pallas as pl
from jax.experimental.pallas import tpu as pltpu

F32 = jnp.float32
BF16 = jnp.bfloat16
EPS = 1e-6
NEG_INF = float("-inf")

LANES_V7X = 128
SUBLANES_V7X = 8
VMEM_LIMIT_BYTES_V7X = 56 * 1024 * 1024

CONV_W = 4
FFN_CONV_W = 3
HALO = SUBLANES_V7X

M_HEAD_DIM = 64
M_GROUPS = 8
M_STATE = 128
SSD_CHUNK = 128
SSD_CHUNKS_PER_STEP = 8
H_EXPAND = 128
HGRN_CHUNK = 64
HGRN_CHUNKS_PER_STEP = 8
G_HEAD_DIM = 128
GDN_CHUNK = 64
GDN_CHUNKS_PER_STEP = 8
GDN_QUAD = 4
X_HEADS = 4
N_MIXERS = 3


def _cparams(sem):
    return pltpu.CompilerParams(dimension_semantics=sem, vmem_limit_bytes=VMEM_LIMIT_BYTES_V7X)


def _dot(a, b):
    return jnp.dot(a, b, preferred_element_type=F32)


def _dot_nt(a, b):
    return lax.dot_general(a, b, (((1,), (1,)), ((), ())), preferred_element_type=F32)


def _dot_tn(a, b):
    return lax.dot_general(a, b, (((0,), (0,)), ((), ())), preferred_element_type=F32)


def _split3(x):
    hi = x.astype(BF16)
    r1 = x - hi.astype(F32)
    mid = r1.astype(BF16)
    lo = (r1 - mid.astype(F32)).astype(BF16)
    return hi, mid, lo


def _dot_sel_lhs(sel, x):
    parts = _split3(x)
    return _dot(jnp.concatenate([sel] * len(parts), axis=1), jnp.concatenate(parts, axis=0))


def _sigmoid(x):
    return 0.5 + 0.5 * jnp.tanh(0.5 * x)


def _silu(x):
    half = 0.5 * x
    return half + half * jnp.tanh(half)


def _softplus(x):
    return jnp.maximum(x, 0.0) + jnp.log1p(jnp.exp(-jnp.abs(x)))


def _rms_scale(x):
    return lax.rsqrt(jnp.mean(x * x, axis=-1, keepdims=True) + EPS)


def _tril_mask(n):
    row = lax.broadcasted_iota(jnp.int32, (n, n), 0)
    col = lax.broadcasted_iota(jnp.int32, (n, n), 1)
    return row >= col


def _as_sel(mask):
    return jnp.where(mask, 1.0, 0.0).astype(BF16)


def _shift_rows_down(x, history, s):
    rows, n = x.shape
    groups = x.reshape(rows // SUBLANES_V7X, SUBLANES_V7X, n)
    rolled = pltpu.roll(groups, s, axis=1)
    hist_rolled = pltpu.roll(history, s, axis=0).reshape(1, SUBLANES_V7X, n)
    prev = jnp.concatenate([hist_rolled, rolled[:-1]], axis=0)
    sub = lax.broadcasted_iota(jnp.int32, rolled.shape, 1)
    return jnp.where(sub < s, prev, rolled).reshape(rows, n)


def _norm_matmul_kernel(x_ref, g_ref, w_ref, o_ref, xn_ref):
    @pl.when(pl.program_id(1) == 0)
    def _():
        x = x_ref[...]
        xn_ref[...] = (x * _rms_scale(x) * g_ref[...]).astype(BF16)

    o_ref[...] = _dot(xn_ref[...], w_ref[...]).astype(o_ref.dtype)


def norm_matmul(x, gain, w, *, tm, tn, layer=0, out_dtype=F32):
    t, d = x.shape
    n = w.shape[2]
    assert t % tm == 0 and n % tn == 0, (t, tm, n, tn)
    return pl.pallas_call(
        _norm_matmul_kernel,
        grid=(t // tm, n // tn),
        in_specs=[
            pl.BlockSpec((tm, d), lambda i, j: (i, 0)),
            pl.BlockSpec((1, d), lambda i, j: (0, 0)),
            pl.BlockSpec((None, d, tn), lambda i, j: (layer, 0, j)),
        ],
        out_specs=pl.BlockSpec((tm, tn), lambda i, j: (i, j)),
        out_shape=jax.ShapeDtypeStruct((t, n), out_dtype),
        scratch_shapes=[pltpu.VMEM((tm, d), BF16)],
        compiler_params=_cparams(("parallel", "arbitrary")),
        name="norm_matmul",
    )(x, gain.reshape(1, d), w)


def _norm_matmul_split_kernel(x_ref, g_ref, w_ref, o16_ref, o32_ref, xn_ref, *, n16_tiles):
    j = pl.program_id(1)

    @pl.when(j == 0)
    def _():
        x = x_ref[...]
        xn_ref[...] = (x * _rms_scale(x) * g_ref[...]).astype(BF16)

    @pl.when(j < n16_tiles)
    def _():
        o16_ref[...] = _dot(xn_ref[...], w_ref[...]).astype(BF16)

    @pl.when(j >= n16_tiles)
    def _():
        o32_ref[...] = _dot(xn_ref[...], w_ref[...])


def norm_matmul_split(x, gain, w, *, n16, tm, tn, layer=0):
    t, d = x.shape
    n = w.shape[2]
    assert t % tm == 0 and n16 % tn == 0 and (n - n16) % tn == 0
    n16_tiles = n16 // tn
    return pl.pallas_call(
        functools.partial(_norm_matmul_split_kernel, n16_tiles=n16_tiles),
        grid=(t // tm, n // tn),
        in_specs=[
            pl.BlockSpec((tm, d), lambda i, j: (i, 0)),
            pl.BlockSpec((1, d), lambda i, j: (0, 0)),
            pl.BlockSpec((None, d, tn), lambda i, j: (layer, 0, j)),
        ],
        out_specs=[
            pl.BlockSpec((tm, tn), lambda i, j: (i, jnp.minimum(j, n16_tiles - 1))),
            pl.BlockSpec((tm, tn), lambda i, j: (i, jnp.maximum(j - n16_tiles, 0))),
        ],
        out_shape=[jax.ShapeDtypeStruct((t, n16), BF16), jax.ShapeDtypeStruct((t, n - n16), F32)],
        scratch_shapes=[pltpu.VMEM((tm, d), BF16)],
        compiler_params=_cparams(("parallel", "arbitrary")),
        name="norm_matmul_split",
    )(x, gain.reshape(1, d), w)


def _matmul_res_kernel(a_ref, w_ref, r_ref, o_ref):
    o_ref[...] = r_ref[...] + _dot(a_ref[...], w_ref[...])


def matmul_residual(a, w, res, *, tm, layer=0):
    t, k = a.shape
    d = w.shape[2]
    assert t % tm == 0
    return pl.pallas_call(
        _matmul_res_kernel,
        grid=(t // tm,),
        in_specs=[
            pl.BlockSpec((tm, k), lambda i: (i, 0)),
            pl.BlockSpec((None, k, d), lambda i: (layer, 0, 0)),
            pl.BlockSpec((tm, d), lambda i: (i, 0)),
        ],
        out_specs=pl.BlockSpec((tm, d), lambda i: (i, 0)),
        out_shape=jax.ShapeDtypeStruct((t, d), F32),
        compiler_params=_cparams(("parallel",)),
        name="matmul_residual",
    )(a, w, res)


def _xattn_kernel(x_ref, g_ref, wq_ref, kv_ref, wo_ref, o_ref, *, heads):
    x = x_ref[...]
    d = x.shape[1]
    dh = d // heads
    xn = (x * _rms_scale(x) * g_ref[...]).astype(BF16)
    q = (_dot(xn, wq_ref[...]) * (dh ** -0.5)).astype(BF16)
    outs = []
    for h in range(heads):
        k_h = kv_ref[:, h * dh:(h + 1) * dh]
        v_h = kv_ref[:, d + h * dh:d + (h + 1) * dh]
        s = _dot_nt(q[:, h * dh:(h + 1) * dh], k_h)
        p = jnp.exp(s - jnp.max(s, axis=-1, keepdims=True))
        o_h = _dot(p.astype(BF16), v_h) / jnp.sum(p, axis=-1, keepdims=True)
        outs.append(o_h.astype(BF16))
    o_ref[...] = x + _dot(jnp.concatenate(outs, axis=1), wo_ref[...])


def memory_cross_attention(x, gain, wq, kv, wo, *, layer, bsz, seq, n_mem, tq):
    t, d = x.shape
    nq = seq // tq
    return pl.pallas_call(
        functools.partial(_xattn_kernel, heads=X_HEADS),
        grid=(bsz, nq),
        in_specs=[
            pl.BlockSpec((tq, d), lambda b, i: (b * nq + i, 0)),
            pl.BlockSpec((1, d), lambda b, i: (0, 0)),
            pl.BlockSpec((None, d, d), lambda b, i: (layer, 0, 0)),
            pl.BlockSpec((n_mem, 2 * d), lambda b, i: (b, 0)),
            pl.BlockSpec((None, d, d), lambda b, i: (layer, 0, 0)),
        ],
        out_specs=pl.BlockSpec((tq, d), lambda b, i: (b * nq + i, 0)),
        out_shape=jax.ShapeDtypeStruct((t, d), F32),
        compiler_params=_cparams(("parallel", "parallel")),
        name="memory_xattn",
    )(x, gain.reshape(1, d), wq, kv, wo)


def _ffn_kernel(x_ref, g_ref, wg_ref, wu_ref, cw_ref, cb_ref, wd_ref, *rest, final_norm):
    if final_norm:
        fg_ref, o_ref, act_ref, halo_ref = rest
    else:
        o_ref, act_ref, halo_ref = rest
    tm = x_ref.shape[0]
    f = wg_ref.shape[1]

    @pl.when(pl.program_id(1) == 0)
    def _():
        halo_ref[...] = jnp.zeros_like(halo_ref)

    x = x_ref[...]
    xn = (x * _rms_scale(x) * g_ref[...]).astype(BF16)
    cols_per = 2 * LANES_V7X
    for c in range(0, f, cols_per):
        cols = slice(c, c + cols_per)
        gate = _dot(xn, wg_ref[:, cols])
        up = _dot(xn, wu_ref[:, cols])
        history = halo_ref[:, cols]
        halo_ref[:, cols] = gate[tm - HALO:tm, :]
        conv = cb_ref[:, cols] + cw_ref[FFN_CONV_W - 1:FFN_CONV_W, cols] * gate
        for tap in range(FFN_CONV_W - 1):
            conv = conv + cw_ref[tap:tap + 1, cols] * _shift_rows_down(gate, history, FFN_CONV_W - 1 - tap)
        act_ref[:, cols] = (_silu(conv) * up).astype(BF16)
    y = x + _dot(act_ref[...], wd_ref[...])
    if final_norm:
        y = y * _rms_scale(y) * fg_ref[...]
    o_ref[...] = y


def conv_glu_ffn(x, gain, w_up, conv_w, conv_b, wd, *, layer, bsz, seq, tm, final_gain=None):
    t, d = x.shape
    f = wd.shape[1]
    assert seq % tm == 0 and f % (2 * LANES_V7X) == 0
    ni = seq // tm
    resident = lambda shape: pl.BlockSpec(shape, lambda b, i: (0, 0), pipeline_mode=pl.Buffered(1))
    stacked = lambda shape, col: pl.BlockSpec((None,) + shape, lambda b, i: (layer, 0, col),
                                              pipeline_mode=pl.Buffered(1))
    in_specs = [
        pl.BlockSpec((tm, d), lambda b, i: (b * ni + i, 0)),
        resident((1, d)),
        stacked((d, f), 0),
        stacked((d, f), 1),
        resident((FFN_CONV_W, f)),
        resident((1, f)),
        stacked((f, d), 0),
    ]
    args = [x, gain.reshape(1, d), w_up, w_up, conv_w, conv_b.reshape(1, f), wd]
    if final_gain is not None:
        in_specs.append(resident((1, d)))
        args.append(final_gain.reshape(1, d))
    return pl.pallas_call(
        functools.partial(_ffn_kernel, final_norm=final_gain is not None),
        grid=(bsz, ni),
        in_specs=in_specs,
        out_specs=pl.BlockSpec((tm, d), lambda b, i: (b * ni + i, 0)),
        out_shape=jax.ShapeDtypeStruct((t, d), F32),
        scratch_shapes=[pltpu.VMEM((tm, f), BF16), pltpu.VMEM((HALO, f), F32)],
        compiler_params=_cparams(("parallel", "arbitrary")),
        name="conv_glu_ffn",
    )(*args)


def _inproj_kernel(x_ref, g_ref, w_ref, cw_ref, cb_ref, ws_ref, o_ref, os_ref, xn_ref, gp_ref, halo_ref,
                   *, n_conv_tiles):
    i = pl.program_id(1)
    j = pl.program_id(2)
    tm = x_ref.shape[0]

    @pl.when(j == 0)
    def _():
        x = x_ref[...]
        xn = (x * _rms_scale(x) * g_ref[...]).astype(BF16)
        xn_ref[...] = xn
        os_ref[...] = _dot(xn, ws_ref[...])

    @pl.when(i == 0)
    def _():
        gp_ref[...] = jnp.zeros_like(gp_ref)

    @pl.when(jnp.logical_and(i > 0, j < n_conv_tiles))
    def _():
        gp_ref[...] = halo_ref[j]

    @pl.when(j < n_conv_tiles)
    def _():
        acc = _dot(xn_ref[...], w_ref[...])
        halo_ref[j] = acc[tm - HALO:tm, :]
        hist = gp_ref[...]
        w0, w1, w2, w3 = (cw_ref[k:k + 1, :] for k in range(CONV_W))
        prev = _shift_rows_down(acc, hist, 1)
        near = cb_ref[...] + w3 * acc + w2 * prev
        far = w1 * acc + w0 * prev
        far_hist = w1 * hist + w0 * pltpu.roll(hist, 1, axis=0)
        o_ref[...] = _silu(near + _shift_rows_down(far, far_hist, 2)).astype(BF16)

    @pl.when(j >= n_conv_tiles)
    def _():
        o_ref[...] = _silu(_dot(xn_ref[...], w_ref[...])).astype(BF16)


def mixer_in_proj(x, gain, w, w_small, conv_w, conv_b, *, bsz, seq, tm, tn):
    t, d = x.shape
    n = w.shape[1]
    conv_dim = conv_w.shape[1]
    ns = w_small.shape[1]
    assert seq % tm == 0 and n % tn == 0 and conv_dim % tn == 0
    ni, nj, n_conv_tiles = seq // tm, n // tn, conv_dim // tn
    conv_col = lambda b, i, j: (0, jnp.minimum(j, n_conv_tiles - 1))
    return pl.pallas_call(
        functools.partial(_inproj_kernel, n_conv_tiles=n_conv_tiles),
        grid=(bsz, ni, nj),
        in_specs=[
            pl.BlockSpec((tm, d), lambda b, i, j: (b * ni + i, 0)),
            pl.BlockSpec((1, d), lambda b, i, j: (0, 0)),
            pl.BlockSpec((d, tn), lambda b, i, j: (0, j)),
            pl.BlockSpec((CONV_W, tn), conv_col),
            pl.BlockSpec((1, tn), conv_col),
            pl.BlockSpec((d, ns), lambda b, i, j: (0, 0)),
        ],
        out_specs=[
            pl.BlockSpec((tm, tn), lambda b, i, j: (b * ni + i, j)),
            pl.BlockSpec((tm, ns), lambda b, i, j: (b * ni + i, 0)),
        ],
        out_shape=[jax.ShapeDtypeStruct((t, n), BF16), jax.ShapeDtypeStruct((t, ns), F32)],
        scratch_shapes=[
            pltpu.VMEM((tm, d), BF16),
            pltpu.VMEM((HALO, tn), F32),
            pltpu.VMEM((n_conv_tiles, HALO, tn), F32),
        ],
        compiler_params=_cparams(("parallel", "arbitrary", "arbitrary")),
        name="mixer_in_proj",
    )(x, gain.reshape(1, d), w, conv_w, conv_b.reshape(1, conv_dim), w_small)


def _split2(x):
    hi = x.astype(BF16)
    return hi, (x - hi.astype(F32)).astype(BF16)


def _select(parts, sel_stacked):
    k = parts[0].shape[1]
    return _dot(jnp.concatenate(parts, axis=1), sel_stacked[0:len(parts) * k, :])


def _ssd_kernel(xbc_ref, zs_ref, dt_ref, dtb_ref, alog_ref, dskip_ref, nw_ref,
                ehead_ref, o_ref, s_ref, *, q, groups, inner, state):
    @pl.when(pl.program_id(1) == 0)
    def _():
        s_ref[...] = jnp.zeros_like(s_ref)

    for sc in range(xbc_ref.shape[0] // q):
        rows = pl.ds(sc * q, q)
        _ssd_chunk(xbc_ref.at[rows], zs_ref.at[rows], dt_ref.at[rows], dtb_ref, alog_ref, dskip_ref, nw_ref,
                   ehead_ref, o_ref.at[rows], s_ref, q=q, groups=groups, inner=inner, state=state)


def _ssd_chunk(xbc_ref, zs_ref, dt_ref, dtb_ref, alog_ref, dskip_ref, nw_ref,
               ehead_ref, o_ref, s_ref, *, q, groups, inner, state):
    gw = inner // groups
    hpg = gw // M_HEAD_DIM
    dt = _softplus(dt_ref[...] + dtb_ref[...])
    a_neg = -jnp.exp(alog_ref[...])
    tril = _tril_mask(q)
    acum = _dot_sel_lhs(_as_sel(tril), dt * a_neg)
    acum_t = acum.T
    dt2 = _split2(dt)
    ac3 = _split3(acum)
    lane_head = lax.broadcasted_iota(jnp.int32, (q, gw), 1) // M_HEAD_DIM
    head_sel = [jnp.where(lane_head == hh, 1.0, 0.0).astype(BF16) for hh in range(hpg)]

    for g in range(groups):
        xs = xbc_ref[:, g * gw:(g + 1) * gw].astype(F32)
        bm16 = xbc_ref[:, inner + g * state:inner + (g + 1) * state]
        c_lo = inner + groups * state + g * state
        cm16 = xbc_ref[:, c_lo:c_lo + state]
        e_g = ehead_ref[:, g * gw:(g + 1) * gw]
        dt_x = _select(dt2, e_g)
        ac_x = _select(ac3, e_g)
        xdt = xs * dt_x
        xdt16 = xdt.astype(BF16)
        cb = _dot_nt(cm16, bm16)
        lhs, rhs = [], []
        for hh in range(hpg):
            h = g * hpg + hh
            diff = jnp.broadcast_to(acum[:, h:h + 1], (q, q)) - acum_t[h:h + 1, :]
            dec = jnp.exp(jnp.where(tril, diff, NEG_INF))
            lhs.append((cb * dec).astype(BF16))
            rhs.append(xdt16 * head_sel[hh])
        y = _dot(jnp.concatenate(lhs, axis=1), jnp.concatenate(rhs, axis=0))
        s_old = s_ref[g]
        y = y + _dot(cm16, s_old.astype(BF16)) * jnp.exp(ac_x)
        y = y + dskip_ref[:, g * gw:(g + 1) * gw] * xs
        last = ac_x[q - 1:q, :]
        xw = (xdt * jnp.exp(last - ac_x)).astype(BF16)
        s_ref[g] = s_old * jnp.exp(last) + _dot_tn(bm16, xw)
        y = y * zs_ref[:, g * gw:(g + 1) * gw].astype(F32)
        o_ref[:, g * gw:(g + 1) * gw] = (y * _rms_scale(y) * nw_ref[:, g * gw:(g + 1) * gw]).astype(BF16)


def ssd_mixer_core(act, dt_raw, dt_bias, a_log, d_skip, norm_w, *, bsz, seq, inner, heads):
    t = act.shape[0]
    q = SSD_CHUNK
    rows = q * SSD_CHUNKS_PER_STEP
    assert seq % rows == 0
    nc = seq // rows
    conv_dim = inner + 2 * M_GROUPS * M_STATE
    assert conv_dim == 2 * inner and heads <= LANES_V7X
    pad = LANES_V7X - heads
    head_ids = jnp.arange(LANES_V7X)
    n_split = 3
    ehead = (head_ids[:, None] == (jnp.arange(inner) // M_HEAD_DIM)[None, :]).astype(BF16)
    ehead = jnp.concatenate([ehead] * n_split, axis=0)
    z_blk = conv_dim // inner
    row = lambda b, c: b * nc + c
    const = lambda shape: pl.BlockSpec(shape, lambda b, c: (0, 0))
    return pl.pallas_call(
        functools.partial(_ssd_kernel, q=q, groups=M_GROUPS, inner=inner, state=M_STATE),
        grid=(bsz, nc),
        in_specs=[
            pl.BlockSpec((rows, conv_dim), lambda b, c: (row(b, c), 0)),
            pl.BlockSpec((rows, inner), lambda b, c: (row(b, c), z_blk)),
            pl.BlockSpec((rows, LANES_V7X), lambda b, c: (row(b, c), 0)),
            const((1, LANES_V7X)),
            const((1, LANES_V7X)),
            const((1, inner)),
            const((1, inner)),
            const((n_split * LANES_V7X, inner)),
        ],
        out_specs=pl.BlockSpec((rows, inner), lambda b, c: (row(b, c), 0)),
        out_shape=jax.ShapeDtypeStruct((t, inner), BF16),
        scratch_shapes=[pltpu.VMEM((M_GROUPS, M_STATE, inner // M_GROUPS), F32)],
        compiler_params=_cparams(("parallel", "arbitrary")),
        name="ssd_core",
    )(act, act, dt_raw,
      jnp.pad(dt_bias, (0, pad)).reshape(1, LANES_V7X), jnp.pad(a_log, (0, pad)).reshape(1, LANES_V7X),
      jnp.repeat(d_skip, M_HEAD_DIM).reshape(1, inner), norm_w.reshape(1, inner), ehead)


def _hgrn2_kernel(q_ref, f_ref, i_ref, g_ref, lb_ref, nw_ref, o_ref, s_ref, *, q, heads):
    c = pl.program_id(1)
    dk = H_EXPAND

    @pl.when(c == 0)
    def _():
        s_ref[...] = jnp.zeros_like(s_ref)

    lb = lb_ref[...]
    tril = _tril_mask(q)
    tril_sel = _as_sel(tril)
    for sc in range(q_ref.shape[0] // q):
        rows = slice(sc * q, (sc + 1) * q)
        forget = lb + (1.0 - lb) * _sigmoid(f_ref[rows, :])
        gc = _dot_sel_lhs(tril_sel, jnp.log(forget))
        key = 1.0 - forget
        qs = _silu(q_ref[rows, :].astype(F32)) * (dk ** -0.5)
        mid = q // 2 - 1
        g_mid = gc[mid:mid + 1, :]
        g_last = gc[q - 1:q, :]
        q_mid = qs * jnp.exp(gc - g_mid)
        k_mid = key * jnp.exp(g_mid - gc)
        q_dec = q_mid.astype(BF16)
        k_inv = k_mid.astype(BF16)
        q_in = (q_mid * jnp.exp(g_mid)).astype(BF16)
        k_end = (k_mid * jnp.exp(g_last - g_mid)).astype(BF16)
        e_last = jnp.exp(g_last)
        for h in range(heads):
            sl = slice(h * dk, (h + 1) * dk)
            v_h = i_ref[rows, sl]
            att = jnp.where(tril, _dot_nt(q_dec[:, sl], k_inv[:, sl]), 0.0)
            st_old = s_ref[h]
            o_h = _dot(att.astype(BF16), v_h) + _dot_nt(q_in[:, sl], st_old.astype(BF16))
            s_ref[h] = st_old * e_last[:, sl] + _dot_tn(v_h, k_end[:, sl])
            o_h = o_h * _rms_scale(o_h) * nw_ref[...]
            o_ref[rows, sl] = (o_h * _silu(g_ref[rows, sl].astype(F32))).astype(BF16)


def hgrn2_mixer_core(qig, f_raw, lower_bound, norm_w, *, bsz, seq, d):
    t = qig.shape[0]
    q = HGRN_CHUNK
    rows = q * HGRN_CHUNKS_PER_STEP
    assert seq % rows == 0
    nc = seq // rows
    heads = d // H_EXPAND
    row = lambda b, c: b * nc + c
    part = lambda k: pl.BlockSpec((rows, d), lambda b, c: (row(b, c), k))
    return pl.pallas_call(
        functools.partial(_hgrn2_kernel, q=q, heads=heads),
        grid=(bsz, nc),
        in_specs=[part(0), part(0), part(1), part(2),
                  pl.BlockSpec((1, d), lambda b, c: (0, 0)),
                  pl.BlockSpec((1, H_EXPAND), lambda b, c: (0, 0))],
        out_specs=pl.BlockSpec((rows, d), lambda b, c: (row(b, c), 0)),
        out_shape=jax.ShapeDtypeStruct((t, d), BF16),
        scratch_shapes=[pltpu.VMEM((heads, H_EXPAND, d // heads), F32)],
        compiler_params=_cparams(("parallel", "arbitrary")),
        name="hgrn2_core",
    )(qig, f_raw, qig, qig, lower_bound.reshape(1, d), norm_w.reshape(1, H_EXPAND))


def _block_diag(blocks):
    n = len(blocks)
    rows = []
    for e, blk in enumerate(blocks):
        zero = jnp.zeros_like(blk)
        rows.append(jnp.concatenate([blk if k == e else zero for k in range(n)], axis=1))
    return jnp.concatenate(rows, axis=0)


def _gdn_kernel(qkv_ref, zs_ref, ba_ref, dtb_ref, alog_ref, nw_ref, o_ref, s_ref, *, q, qk_heads, v_heads):
    @pl.when(pl.program_id(1) == 0)
    def _():
        s_ref[...] = jnp.zeros_like(s_ref)

    for sc in range(qkv_ref.shape[0] // q):
        rows = pl.ds(sc * q, q)
        _gdn_chunk(qkv_ref.at[rows], zs_ref.at[rows], ba_ref.at[rows], dtb_ref, alog_ref, nw_ref,
                   o_ref.at[rows], s_ref, q=q, qk_heads=qk_heads, v_heads=v_heads)


def _gdn_chunk(qkv_ref, zs_ref, ba_ref, dtb_ref, alog_ref, nw_ref, o_ref, s_ref, *, q, qk_heads, v_heads):
    dh = G_HEAD_DIM
    key_dim = qk_heads * dh
    rep = v_heads // qk_heads
    assert rep == 2 and GDN_QUAD % rep == 0 and 2 * q == LANES_V7X
    tril = _tril_mask(q)
    beta = _sigmoid(ba_ref[:, 0:LANES_V7X])
    gate = -jnp.exp(alog_ref[...]) * _softplus(ba_ref[:, LANES_V7X:2 * LANES_V7X] + dtb_ref[...])
    gc = _dot_sel_lhs(_as_sel(tril), gate)

    eye_t = (lax.broadcasted_iota(jnp.int32, (q, v_heads * q), 0)
             == lax.broadcasted_iota(jnp.int32, (q, v_heads * q), 1) % q)
    col_s = lax.broadcasted_iota(jnp.int32, (q, v_heads * q), 1) % q
    row_l = lax.broadcasted_iota(jnp.int32, (q, v_heads * q), 0)
    lane_lo = lax.broadcasted_iota(jnp.int32, (q, 2 * q), 1) < q
    g_wide = [jnp.broadcast_to(gc[:, h:h + 1], (q, dh)) for h in range(v_heads)]
    b_wide = [jnp.broadcast_to(beta[:, h:h + 1], (q, dh)) for h in range(v_heads)]
    g_col = jnp.concatenate([jnp.where(lane_lo, g_wide[h], g_wide[h + 1]) for h in range(0, v_heads, 2)], axis=1)
    b_col = jnp.concatenate([jnp.where(lane_lo, b_wide[h], b_wide[h + 1]) for h in range(0, v_heads, 2)], axis=1)
    g_row = jnp.sum(jnp.where(eye_t, g_col, 0.0), axis=0, keepdims=True)
    decay = jnp.exp(jnp.where(row_l >= col_s, g_col - g_row, NEG_INF))

    kk_parts, qk_parts, k_heads, k_f32, q_heads = [], [], [], [], []
    for p in range(qk_heads):
        q_p = qkv_ref[:, p * dh:(p + 1) * dh].astype(F32)
        k_p = qkv_ref[:, key_dim + p * dh:key_dim + (p + 1) * dh].astype(F32)
        q_p = q_p * lax.rsqrt(jnp.sum(q_p * q_p, axis=-1, keepdims=True) + EPS) * (dh ** -0.5)
        k_p = k_p * lax.rsqrt(jnp.sum(k_p * k_p, axis=-1, keepdims=True) + EPS)
        k16 = k_p.astype(BF16)
        q16 = q_p.astype(BF16)
        k_rep = jnp.concatenate([k16] * rep, axis=0)
        kk_parts.append(_dot_nt(k16, k_rep))
        qk_parts.append(_dot_nt(q16, k_rep))
        k_heads.append(k16)
        k_f32.append(k_p)
        q_heads.append(q16)
    kk = jnp.concatenate(kk_parts, axis=1)
    qk = jnp.concatenate(qk_parts, axis=1)
    m_all = jnp.where(row_l > col_s, b_col * kk * decay, 0.0)
    att_all = qk * decay

    quad_w = GDN_QUAD * q
    n_quads = v_heads // GDN_QUAD
    lane_blk = lax.broadcasted_iota(jnp.int32, (q, quad_w), 1) // q
    blk_sel = [jnp.where(lane_blk == e, 1.0, 0.0).astype(BF16) for e in range(GDN_QUAD)]

    def blocks_on_diagonal(x16):
        return jnp.concatenate([x16 * blk_sel[e] for e in range(GDN_QUAD)], axis=0)

    def times_blockwise(lhs_hi, lhs_lo, bd_hi, bd_lo):
        lhs = jnp.concatenate([jnp.concatenate([hi, lo, hi], axis=1) for hi, lo in zip(lhs_hi, lhs_lo)], axis=0)
        out = _dot(lhs, jnp.concatenate([bd_hi, bd_hi, bd_lo], axis=0))
        return [out[i * q:(i + 1) * q, :] for i in range(len(lhs_hi))]

    levels = q.bit_length() - 2
    p_cur = [-m_all[:, u * quad_w:(u + 1) * quad_w] for u in range(n_quads)]
    n_acc = list(p_cur)
    for lvl in range(levels + 1):
        for u in range(n_quads):
            p_hi, p_lo = _split2(p_cur[u])
            bd_hi, bd_lo = blocks_on_diagonal(p_hi), blocks_on_diagonal(p_lo)
            if lvl == 0:
                (p_cur[u],) = times_blockwise([p_hi], [p_lo], bd_hi, bd_lo)
            elif lvl < levels:
                n_hi, n_lo = _split2(n_acc[u])
                n_p, p_sq = times_blockwise([n_hi, p_hi], [n_lo, p_lo], bd_hi, bd_lo)
                n_acc[u] = n_acc[u] + p_cur[u] + n_p
                p_cur[u] = p_sq
            else:
                n_hi, n_lo = _split2(n_acc[u])
                (n_p,) = times_blockwise([n_hi], [n_lo], bd_hi, bd_lo)
                n_acc[u] = n_acc[u] + p_cur[u] + n_p

    def two_pass(a, rhs16):
        a_hi, a_lo = _split2(a)
        return _dot(jnp.concatenate([a_hi, a_lo], axis=1), jnp.concatenate([rhs16, rhs16], axis=0))

    b_row = jnp.sum(jnp.where(eye_t, b_col, 0.0), axis=0, keepdims=True)
    bg_row = b_row * jnp.exp(g_row)
    eye_f = jnp.where(eye_t, 1.0, 0.0)
    zero_blk = jnp.zeros((dh, dh), BF16)
    pairs_per_quad = GDN_QUAD // rep
    for u in range(n_quads):
        cols = slice(u * quad_w, (u + 1) * quad_w)
        t_b = (n_acc[u] + eye_f[:, cols]) * b_row[:, cols]
        t_bg = (n_acc[u] + eye_f[:, cols]) * bg_row[:, cols]
        v16 = [qkv_ref[:, 2 * key_dim + h * dh:2 * key_dim + (h + 1) * dh]
               for h in range(u * GDN_QUAD, (u + 1) * GDN_QUAD)]
        k16 = [k_heads[h // rep] for h in range(u * GDN_QUAD, (u + 1) * GDN_QUAD)]
        u_quad = two_pass(t_b, _block_diag(v16))
        w_quad = two_pass(t_bg, _block_diag(k16))
        v_new, o_inter = [], []
        for pi in range(pairs_per_quad):
            p = u * pairs_per_quad + pi
            pc = slice(pi * rep * dh, (pi + 1) * rep * dh)
            g_pair = jnp.concatenate(g_wide[p * rep:(p + 1) * rep], axis=1)
            g_last = g_pair[q - 1:q, :]
            s_old = s_ref[p]
            s16 = s_old.astype(BF16)
            s_bd = jnp.concatenate([jnp.concatenate([s16[:, 0:dh], zero_blk], axis=1),
                                    jnp.concatenate([zero_blk, s16[:, dh:2 * dh]], axis=1)], axis=0)
            lhs = jnp.concatenate([w_quad[:, pc].astype(BF16), jnp.concatenate([q_heads[p]] * rep, axis=1)],
                                  axis=0)
            both = _dot(lhs, s_bd)
            vn = u_quad[:, pc] - both[0:q, :]
            o_inter.append(both[q:2 * q, :] * jnp.exp(g_pair))
            vn16 = vn.astype(BF16)
            zero_q = jnp.zeros((q, dh), BF16)
            vn_bd = jnp.concatenate([jnp.concatenate([vn16[:, 0:dh], zero_q], axis=1),
                                     jnp.concatenate([zero_q, vn16[:, dh:2 * dh]], axis=1)], axis=0)
            k_dec = jnp.concatenate([k_f32[p]] * rep, axis=1) * jnp.exp(g_last - g_pair)
            ke = jnp.concatenate([k_dec[:, 0:dh], k_dec[:, dh:2 * dh]], axis=0).astype(BF16)
            s_ref[p] = s_old * jnp.exp(g_last) + _dot_tn(ke, vn_bd)
            v_new.extend([vn16[:, 0:dh], vn16[:, dh:2 * dh]])
        o_intra = _dot(att_all[:, cols].astype(BF16), _block_diag(v_new))
        o_quad = jnp.concatenate(o_inter, axis=1) + o_intra
        for e in range(GDN_QUAD):
            h = u * GDN_QUAD + e
            sl = slice(h * dh, (h + 1) * dh)
            o_h = o_quad[:, e * dh:(e + 1) * dh]
            o_h = o_h * _rms_scale(o_h) * nw_ref[...]
            o_ref[:, sl] = (o_h * zs_ref[:, sl].astype(F32)).astype(BF16)


def gdn_mixer_core(act, ba_raw, a_log, dt_bias, norm_w, *, bsz, seq, qk_heads, v_heads):
    t = act.shape[0]
    q = GDN_CHUNK
    rows = q * GDN_CHUNKS_PER_STEP
    assert seq % rows == 0
    nc = seq // rows
    dh = G_HEAD_DIM
    key_dim = qk_heads * dh
    val_dim = v_heads * dh
    conv_dim = 2 * key_dim + val_dim
    assert conv_dim == 2 * val_dim and v_heads % GDN_QUAD == 0
    pad = LANES_V7X - v_heads
    z_blk = conv_dim // val_dim
    row = lambda b, c: b * nc + c
    const = lambda shape: pl.BlockSpec(shape, lambda b, c: (0, 0))
    return pl.pallas_call(
        functools.partial(_gdn_kernel, q=q, qk_heads=qk_heads, v_heads=v_heads),
        grid=(bsz, nc),
        in_specs=[
            pl.BlockSpec((rows, conv_dim), lambda b, c: (row(b, c), 0)),
            pl.BlockSpec((rows, val_dim), lambda b, c: (row(b, c), z_blk)),
            pl.BlockSpec((rows, 2 * LANES_V7X), lambda b, c: (row(b, c), 0)),
            const((1, LANES_V7X)),
            const((1, LANES_V7X)),
            const((1, dh)),
        ],
        out_specs=pl.BlockSpec((rows, val_dim), lambda b, c: (row(b, c), 0)),
        out_shape=jax.ShapeDtypeStruct((t, val_dim), BF16),
        scratch_shapes=[pltpu.VMEM((qk_heads, dh, (v_heads // qk_heads) * dh), F32)],
        compiler_params=_cparams(("parallel", "arbitrary")),
        name="gdn_core",
    )(act, act, ba_raw, jnp.pad(dt_bias, (0, pad)).reshape(1, LANES_V7X),
      jnp.pad(a_log, (0, pad)).reshape(1, LANES_V7X), norm_w.reshape(1, dh))


def _pad_cols(w, width):
    return jnp.pad(w, ((0, 0), (0, width - w.shape[1])))


def _ssd_in_weights(in_w, inner, conv_dim):
    z = in_w[:, :inner]
    xbc = in_w[:, inner:inner + conv_dim]
    dt = _pad_cols(in_w[:, inner + conv_dim:], LANES_V7X)
    return jnp.concatenate([xbc, z], axis=1).astype(BF16), dt.astype(BF16)


def _gdn_in_weights(in_w, conv_dim, val_dim, v_heads):
    main = in_w[:, :conv_dim + val_dim]
    b = _pad_cols(in_w[:, conv_dim + val_dim:conv_dim + val_dim + v_heads], LANES_V7X)
    a = _pad_cols(in_w[:, conv_dim + val_dim + v_heads:], LANES_V7X)
    return main.astype(BF16), jnp.concatenate([b, a], axis=1).astype(BF16)


def _pick_tile(n, candidates):
    for cand in candidates:
        if n % cand == 0:
            return cand
    raise ValueError(f"no tile for {n} in {candidates}")


def kernel(x, mem, ln_mix, ln_xattn, ln_mem, ln_ffn, final_norm, m_in_w, m_conv_w, m_conv_b, m_dt_bias, m_a_log, m_d, m_norm_w, m_out_w, h_in_w, h_lower_bounds, h_norm_w, h_out_w, g_in_w, g_conv_w, g_a_log, g_dt_bias, g_norm_w, g_out_w, xa_q, xa_kv, xa_o, f_up, f_conv_w, f_conv_b, f_down):
    bsz, seq, d = x.shape
    n_mem = mem.shape[1]
    depth = ln_mix.shape[0]
    t = bsz * seq
    tm = _pick_tile(t, (1024, 512, 256, 128, 64))
    tq = _pick_tile(seq, (512, 256, 128, 64))
    ts = _pick_tile(seq, (1024, 512, 256, 128, 64))

    lb = jnp.cumsum(jax.nn.softmax(h_lower_bounds.astype(F32), axis=0), axis=0)
    lb = lb - lb[:1]

    m_inner = m_out_w.shape[1]
    m_heads = m_dt_bias.shape[1]
    m_conv_dim = m_conv_w.shape[2]
    g_v_heads = g_a_log.shape[1]
    g_val_dim = g_out_w.shape[1]
    g_conv_dim = g_conv_w.shape[2]
    g_qk_heads = (g_conv_dim - g_val_dim) // (2 * G_HEAD_DIM)

    xa_q16, xa_kv16, xa_o16 = (w.astype(BF16) for w in (xa_q, xa_kv, xa_o))
    h_in16 = jnp.concatenate([h_in_w[..., :d], h_in_w[..., 2 * d:], h_in_w[..., d:2 * d]], axis=-1).astype(BF16)
    f_up16, f_down16 = f_up.astype(BF16), f_down.astype(BF16)
    m_out16, h_out16, g_out16 = m_out_w.astype(BF16), h_out_w.astype(BF16), g_out_w.astype(BF16)

    xf = x.reshape(t, d)
    memf = mem.reshape(bsz * n_mem, d)
    ia = ib = ic = 0
    for i in range(depth):
        kind = i % N_MIXERS
        if kind == 0:
            w_in, w_dt = _ssd_in_weights(m_in_w[ia], m_inner, m_conv_dim)
            act, dt_raw = mixer_in_proj(xf, ln_mix[i], w_in, w_dt, m_conv_w[ia], m_conv_b[ia],
                                        bsz=bsz, seq=seq, tm=ts, tn=1024)
            y = ssd_mixer_core(act, dt_raw, m_dt_bias[ia], m_a_log[ia], m_d[ia],
                               m_norm_w[ia], bsz=bsz, seq=seq, inner=m_inner, heads=m_heads)
            xf = matmul_residual(y, m_out16, xf, tm=ts, layer=ia)
            ia += 1
        elif kind == 1:
            qig, f_raw = norm_matmul_split(xf, ln_mix[i], h_in16, n16=3 * d, layer=ib, tm=tm, tn=1024)
            y = hgrn2_mixer_core(qig, f_raw, lb[i], h_norm_w[ib], bsz=bsz, seq=seq, d=d)
            xf = matmul_residual(y, h_out16, xf, tm=ts, layer=ib)
            ib += 1
        else:
            w_in, w_ba = _gdn_in_weights(g_in_w[ic], g_conv_dim, g_val_dim, g_v_heads)
            act, ba_raw = mixer_in_proj(xf, ln_mix[i], w_in, w_ba, g_conv_w[ic], jnp.zeros((g_conv_dim,), F32),
                                        bsz=bsz, seq=seq, tm=ts, tn=1024)
            y = gdn_mixer_core(act, ba_raw, g_a_log[ic], g_dt_bias[ic], g_norm_w[ic],
                               bsz=bsz, seq=seq, qk_heads=g_qk_heads, v_heads=g_v_heads)
            xf = matmul_residual(y, g_out16, xf, tm=ts, layer=ic)
            ic += 1

        kv = norm_matmul(memf, ln_mem[i], xa_kv16, layer=i, tm=_pick_tile(bsz * n_mem, (1024, 512, 256)),
                         tn=1024, out_dtype=BF16)
        xf = memory_cross_attention(xf, ln_xattn[i], xa_q16, kv, xa_o16, layer=i,
                                    bsz=bsz, seq=seq, n_mem=n_mem, tq=ts)

        last = i == depth - 1
        xf = conv_glu_ffn(xf, ln_ffn[i], f_up16, f_conv_w[i], f_conv_b[i], f_down16, layer=i,
                          bsz=bsz, seq=seq, tm=tq, final_gain=final_norm if last else None)
    return xf.reshape(bsz, seq, d)
```

```python
import functools

import jax
import jax.numpy as jnp
from jax import lax
from jax.experimental import pallas as pl
from jax.experimental.pallas import tpu as pltpu

F32 = jnp.float32
BF16 = jnp.bfloat16
EPS = 1e-6
NEG_INF = float("-inf")

LANES_V7X = 128
SUBLANES_V7X = 8
VMEM_LIMIT_BYTES_V7X = 56 * 1024 * 1024

CONV_W = 4
FFN_CONV_W = 3
HALO = SUBLANES_V7X

M_HEAD_DIM = 64
M_GROUPS = 8
M_STATE = 128
SSD_CHUNK = 128
SSD_CHUNKS_PER_STEP = 8
H_EXPAND = 128
HGRN_CHUNK = 64
HGRN_CHUNKS_PER_STEP = 8
G_HEAD_DIM = 128
GDN_CHUNK = 64
GDN_CHUNKS_PER_STEP = 8
GDN_QUAD = 4
X_HEADS = 4
N_MIXERS = 3


def _cparams(sem):
    return pltpu.CompilerParams(dimension_semantics=sem, vmem_limit_bytes=VMEM_LIMIT_BYTES_V7X)


def _dot(a, b):
    return jnp.dot(a, b, preferred_element_type=F32)


def _dot_nt(a, b):
    return lax.dot_general(a, b, (((1,), (1,)), ((), ())), preferred_element_type=F32)


def _dot_tn(a, b):
    return lax.dot_general(a, b, (((0,), (0,)), ((), ())), preferred_element_type=F32)


def _split3(x):
    hi = x.astype(BF16)
    r1 = x - hi.astype(F32)
    mid = r1.astype(BF16)
    lo = (r1 - mid.astype(F32)).astype(BF16)
    return hi, mid, lo


def _dot_sel_lhs(sel, x):
    parts = _split3(x)
    return _dot(jnp.concatenate([sel] * len(parts), axis=1), jnp.concatenate(parts, axis=0))


def _sigmoid(x):
    return 0.5 + 0.5 * jnp.tanh(0.5 * x)


def _silu(x):
    half = 0.5 * x
    return half + half * jnp.tanh(half)


def _softplus(x):
    return jnp.maximum(x, 0.0) + jnp.log1p(jnp.exp(-jnp.abs(x)))


def _rms_scale(x):
    return lax.rsqrt(jnp.mean(x * x, axis=-1, keepdims=True) + EPS)


def _tril_mask(n):
    row = lax.broadcasted_iota(jnp.int32, (n, n), 0)
    col = lax.broadcasted_iota(jnp.int32, (n, n), 1)
    return row >= col


def _as_sel(mask):
    return jnp.where(mask, 1.0, 0.0).astype(BF16)


def _shift_rows_down(x, history, s):
    rows, n = x.shape
    groups = x.reshape(rows // SUBLANES_V7X, SUBLANES_V7X, n)
    rolled = pltpu.roll(groups, s, axis=1)
    hist_rolled = pltpu.roll(history, s, axis=0).reshape(1, SUBLANES_V7X, n)
    prev = jnp.concatenate([hist_rolled, rolled[:-1]], axis=0)
    sub = lax.broadcasted_iota(jnp.int32, rolled.shape, 1)
    return jnp.where(sub < s, prev, rolled).reshape(rows, n)


def _norm_matmul_kernel(x_ref, g_ref, w_ref, o_ref, xn_ref):
    @pl.when(pl.program_id(1) == 0)
    def _():
        x = x_ref[...]
        xn_ref[...] = (x * _rms_scale(x) * g_ref[...]).astype(BF16)

    o_ref[...] = _dot(xn_ref[...], w_ref[...]).astype(o_ref.dtype)


def norm_matmul(x, gain, w, *, tm, tn, layer=0, out_dtype=F32):
    t, d = x.shape
    n = w.shape[2]
    assert t % tm == 0 and n % tn == 0, (t, tm, n, tn)
    return pl.pallas_call(
        _norm_matmul_kernel,
        grid=(t // tm, n // tn),
        in_specs=[
            pl.BlockSpec((tm, d), lambda i, j: (i, 0)),
            pl.BlockSpec((1, d), lambda i, j: (0, 0)),
            pl.BlockSpec((None, d, tn), lambda i, j: (layer, 0, j)),
        ],
        out_specs=pl.BlockSpec((tm, tn), lambda i, j: (i, j)),
        out_shape=jax.ShapeDtypeStruct((t, n), out_dtype),
        scratch_shapes=[pltpu.VMEM((tm, d), BF16)],
        compiler_params=_cparams(("parallel", "arbitrary")),
        name="norm_matmul",
    )(x, gain.reshape(1, d), w)


def _norm_matmul_split_kernel(x_ref, g_ref, w_ref, o16_ref, o32_ref, xn_ref, *, n16_tiles):
    j = pl.program_id(1)

    @pl.when(j == 0)
    def _():
        x = x_ref[...]
        xn_ref[...] = (x * _rms_scale(x) * g_ref[...]).astype(BF16)

    @pl.when(j < n16_tiles)
    def _():
        o16_ref[...] = _dot(xn_ref[...], w_ref[...]).astype(BF16)

    @pl.when(j >= n16_tiles)
    def _():
        o32_ref[...] = _dot(xn_ref[...], w_ref[...])


def norm_matmul_split(x, gain, w, *, n16, tm, tn, layer=0):
    t, d = x.shape
    n = w.shape[2]
    assert t % tm == 0 and n16 % tn == 0 and (n - n16) % tn == 0
    n16_tiles = n16 // tn
    return pl.pallas_call(
        functools.partial(_norm_matmul_split_kernel, n16_tiles=n16_tiles),
        grid=(t // tm, n // tn),
        in_specs=[
            pl.BlockSpec((tm, d), lambda i, j: (i, 0)),
            pl.BlockSpec((1, d), lambda i, j: (0, 0)),
            pl.BlockSpec((None, d, tn), lambda i, j: (layer, 0, j)),
        ],
        out_specs=[
            pl.BlockSpec((tm, tn), lambda i, j: (i, jnp.minimum(j, n16_tiles - 1))),
            pl.BlockSpec((tm, tn), lambda i, j: (i, jnp.maximum(j - n16_tiles, 0))),
        ],
        out_shape=[jax.ShapeDtypeStruct((t, n16), BF16), jax.ShapeDtypeStruct((t, n - n16), F32)],
        scratch_shapes=[pltpu.VMEM((tm, d), BF16)],
        compiler_params=_cparams(("parallel", "arbitrary")),
        name="norm_matmul_split",
    )(x, gain.reshape(1, d), w)


def _xattn_kernel(r_ref, y_ref, wout_ref, g_ref, wq_ref, kv_ref, wo_ref, o_ref, *, heads):
    x = r_ref[...] + _dot(y_ref[...], wout_ref[...])
    d = x.shape[1]
    dh = d // heads
    xn = (x * _rms_scale(x) * g_ref[...]).astype(BF16)
    q = (_dot(xn, wq_ref[...]) * (dh ** -0.5)).astype(BF16)
    outs = []
    for h in range(heads):
        k_h = kv_ref[:, h * dh:(h + 1) * dh]
        v_h = kv_ref[:, d + h * dh:d + (h + 1) * dh]
        s = _dot_nt(q[:, h * dh:(h + 1) * dh], k_h)
        p = jnp.exp(s - jnp.max(s, axis=-1, keepdims=True))
        o_h = _dot(p.astype(BF16), v_h) / jnp.sum(p, axis=-1, keepdims=True)
        outs.append(o_h.astype(BF16))
    o_ref[...] = x + _dot(jnp.concatenate(outs, axis=1), wo_ref[...])


def outproj_cross_attention(res, y, w_out, out_layer, gain, wq, kv, wo, *, layer, bsz, seq, n_mem, tq):
    t, d = res.shape
    k = y.shape[1]
    nq = seq // tq
    weight = lambda shape, idx: pl.BlockSpec((None,) + shape, lambda b, i: (idx, 0, 0),
                                             pipeline_mode=pl.Buffered(1))
    return pl.pallas_call(
        functools.partial(_xattn_kernel, heads=X_HEADS),
        grid=(bsz, nq),
        in_specs=[
            pl.BlockSpec((tq, d), lambda b, i: (b * nq + i, 0)),
            pl.BlockSpec((tq, k), lambda b, i: (b * nq + i, 0)),
            weight((k, d), out_layer),
            pl.BlockSpec((1, d), lambda b, i: (0, 0)),
            weight((d, d), layer),
            pl.BlockSpec((n_mem, 2 * d), lambda b, i: (b, 0)),
            weight((d, d), layer),
        ],
        out_specs=pl.BlockSpec((tq, d), lambda b, i: (b * nq + i, 0)),
        out_shape=jax.ShapeDtypeStruct((t, d), F32),
        compiler_params=_cparams(("parallel", "parallel")),
        name="outproj_xattn",
    )(res, y, w_out, gain.reshape(1, d), wq, kv, wo)


def _ffn_kernel(x_ref, g_ref, wg_ref, wu_ref, cw_ref, cb_ref, wd_ref, *rest, final_norm):
    if final_norm:
        fg_ref, o_ref, act_ref, halo_ref = rest
    else:
        o_ref, act_ref, halo_ref = rest
    tm = x_ref.shape[0]
    f = wg_ref.shape[1]

    @pl.when(pl.program_id(1) == 0)
    def _():
        halo_ref[...] = jnp.zeros_like(halo_ref)

    x = x_ref[...]
    xn = (x * _rms_scale(x) * g_ref[...]).astype(BF16)
    cols_per = 2 * LANES_V7X
    for c in range(0, f, cols_per):
        cols = slice(c, c + cols_per)
        gate = _dot(xn, wg_ref[:, cols])
        up = _dot(xn, wu_ref[:, cols])
        history = halo_ref[:, cols]
        halo_ref[:, cols] = gate[tm - HALO:tm, :]
        conv = cb_ref[:, cols] + cw_ref[FFN_CONV_W - 1:FFN_CONV_W, cols] * gate
        for tap in range(FFN_CONV_W - 1):
            conv = conv + cw_ref[tap:tap + 1, cols] * _shift_rows_down(gate, history, FFN_CONV_W - 1 - tap)
        act_ref[:, cols] = (_silu(conv) * up).astype(BF16)
    y = x + _dot(act_ref[...], wd_ref[...])
    if final_norm:
        y = y * _rms_scale(y) * fg_ref[...]
    o_ref[...] = y


def conv_glu_ffn(x, gain, w_up, conv_w, conv_b, wd, *, layer, bsz, seq, tm, final_gain=None):
    t, d = x.shape
    f = wd.shape[1]
    assert seq % tm == 0 and f % (2 * LANES_V7X) == 0
    ni = seq // tm
    resident = lambda shape: pl.BlockSpec(shape, lambda b, i: (0, 0), pipeline_mode=pl.Buffered(1))
    stacked = lambda shape, col: pl.BlockSpec((None,) + shape, lambda b, i: (layer, 0, col),
                                              pipeline_mode=pl.Buffered(1))
    in_specs = [
        pl.BlockSpec((tm, d), lambda b, i: (b * ni + i, 0)),
        resident((1, d)),
        stacked((d, f), 0),
        stacked((d, f), 1),
        resident((FFN_CONV_W, f)),
        resident((1, f)),
        stacked((f, d), 0),
    ]
    args = [x, gain.reshape(1, d), w_up, w_up, conv_w, conv_b.reshape(1, f), wd]
    if final_gain is not None:
        in_specs.append(resident((1, d)))
        args.append(final_gain.reshape(1, d))
    return pl.pallas_call(
        functools.partial(_ffn_kernel, final_norm=final_gain is not None),
        grid=(bsz, ni),
        in_specs=in_specs,
        out_specs=pl.BlockSpec((tm, d), lambda b, i: (b * ni + i, 0)),
        out_shape=jax.ShapeDtypeStruct((t, d), F32),
        scratch_shapes=[pltpu.VMEM((tm, f), BF16), pltpu.VMEM((HALO, f), F32)],
        compiler_params=_cparams(("parallel", "arbitrary")),
        name="conv_glu_ffn",
    )(*args)


def _inproj_kernel(x_ref, g_ref, w_ref, cw_ref, cb_ref, ws_ref, o_ref, os_ref, xn_ref, gp_ref, halo_ref,
                   *, n_conv_tiles):
    i = pl.program_id(1)
    j = pl.program_id(2)
    tm = x_ref.shape[0]

    @pl.when(j == 0)
    def _():
        x = x_ref[...]
        xn = (x * _rms_scale(x) * g_ref[...]).astype(BF16)
        xn_ref[...] = xn
        os_ref[...] = _dot(xn, ws_ref[...])

    @pl.when(i == 0)
    def _():
        gp_ref[...] = jnp.zeros_like(gp_ref)

    @pl.when(jnp.logical_and(i > 0, j < n_conv_tiles))
    def _():
        gp_ref[...] = halo_ref[j]

    @pl.when(j < n_conv_tiles)
    def _():
        acc = _dot(xn_ref[...], w_ref[...])
        halo_ref[j] = acc[tm - HALO:tm, :]
        hist = gp_ref[...]
        w0, w1, w2, w3 = (cw_ref[k:k + 1, :] for k in range(CONV_W))
        prev = _shift_rows_down(acc, hist, 1)
        near = cb_ref[...] + w3 * acc + w2 * prev
        far = w1 * acc + w0 * prev
        far_hist = w1 * hist + w0 * pltpu.roll(hist, 1, axis=0)
        o_ref[...] = _silu(near + _shift_rows_down(far, far_hist, 2)).astype(BF16)

    @pl.when(j >= n_conv_tiles)
    def _():
        o_ref[...] = _silu(_dot(xn_ref[...], w_ref[...])).astype(BF16)


def mixer_in_proj(x, gain, w, w_small, conv_w, conv_b, *, bsz, seq, tm, tn):
    t, d = x.shape
    n = w.shape[1]
    conv_dim = conv_w.shape[1]
    ns = w_small.shape[1]
    assert seq % tm == 0 and n % tn == 0 and conv_dim % tn == 0
    ni, nj, n_conv_tiles = seq // tm, n // tn, conv_dim // tn
    conv_col = lambda b, i, j: (0, jnp.minimum(j, n_conv_tiles - 1))
    return pl.pallas_call(
        functools.partial(_inproj_kernel, n_conv_tiles=n_conv_tiles),
        grid=(bsz, ni, nj),
        in_specs=[
            pl.BlockSpec((tm, d), lambda b, i, j: (b * ni + i, 0)),
            pl.BlockSpec((1, d), lambda b, i, j: (0, 0)),
            pl.BlockSpec((d, tn), lambda b, i, j: (0, j)),
            pl.BlockSpec((CONV_W, tn), conv_col),
            pl.BlockSpec((1, tn), conv_col),
            pl.BlockSpec((d, ns), lambda b, i, j: (0, 0)),
        ],
        out_specs=[
            pl.BlockSpec((tm, tn), lambda b, i, j: (b * ni + i, j)),
            pl.BlockSpec((tm, ns), lambda b, i, j: (b * ni + i, 0)),
        ],
        out_shape=[jax.ShapeDtypeStruct((t, n), BF16), jax.ShapeDtypeStruct((t, ns), F32)],
        scratch_shapes=[
            pltpu.VMEM((tm, d), BF16),
            pltpu.VMEM((HALO, tn), F32),
            pltpu.VMEM((n_conv_tiles, HALO, tn), F32),
        ],
        compiler_params=_cparams(("parallel", "arbitrary", "arbitrary")),
        name="mixer_in_proj",
    )(x, gain.reshape(1, d), w, conv_w, conv_b.reshape(1, conv_dim), w_small)


def _split2(x):
    hi = x.astype(BF16)
    return hi, (x - hi.astype(F32)).astype(BF16)


def _select(parts, sel_stacked):
    k = parts[0].shape[1]
    return _dot(jnp.concatenate(parts, axis=1), sel_stacked[0:len(parts) * k, :])


def _ssd_kernel(xbc_ref, zs_ref, dt_ref, dtb_ref, alog_ref, dskip_ref, nw_ref,
                ehead_ref, o_ref, s_ref, *, q, groups, inner, state):
    @pl.when(pl.program_id(1) == 0)
    def _():
        s_ref[...] = jnp.zeros_like(s_ref)

    for sc in range(xbc_ref.shape[0] // q):
        rows = pl.ds(sc * q, q)
        _ssd_chunk(xbc_ref.at[rows], zs_ref.at[rows], dt_ref.at[rows], dtb_ref, alog_ref, dskip_ref, nw_ref,
                   ehead_ref, o_ref.at[rows], s_ref, q=q, groups=groups, inner=inner, state=state)


def _ssd_chunk(xbc_ref, zs_ref, dt_ref, dtb_ref, alog_ref, dskip_ref, nw_ref,
               ehead_ref, o_ref, s_ref, *, q, groups, inner, state):
    gw = inner // groups
    hpg = gw // M_HEAD_DIM
    dt = _softplus(dt_ref[...] + dtb_ref[...])
    a_neg = -jnp.exp(alog_ref[...])
    tril = _tril_mask(q)
    acum = _dot_sel_lhs(_as_sel(tril), dt * a_neg)
    acum_t = acum.T
    dt2 = _split2(dt)
    ac3 = _split3(acum)
    lane_head = lax.broadcasted_iota(jnp.int32, (q, gw), 1) // M_HEAD_DIM
    head_sel = [jnp.where(lane_head == hh, 1.0, 0.0).astype(BF16) for hh in range(hpg)]

    for g in range(groups):
        xs = xbc_ref[:, g * gw:(g + 1) * gw].astype(F32)
        bm16 = xbc_ref[:, inner + g * state:inner + (g + 1) * state]
        c_lo = inner + groups * state + g * state
        cm16 = xbc_ref[:, c_lo:c_lo + state]
        e_g = ehead_ref[:, g * gw:(g + 1) * gw]
        dt_x = _select(dt2, e_g)
        ac_x = _select(ac3, e_g)
        xdt = xs * dt_x
        xdt16 = xdt.astype(BF16)
        cb = _dot_nt(cm16, bm16)
        lhs, rhs = [], []
        for hh in range(hpg):
            h = g * hpg + hh
            diff = jnp.broadcast_to(acum[:, h:h + 1], (q, q)) - acum_t[h:h + 1, :]
            dec = jnp.exp(jnp.where(tril, diff, NEG_INF))
            lhs.append((cb * dec).astype(BF16))
            rhs.append(xdt16 * head_sel[hh])
        y = _dot(jnp.concatenate(lhs, axis=1), jnp.concatenate(rhs, axis=0))
        s_old = s_ref[g]
        y = y + _dot(cm16, s_old.astype(BF16)) * jnp.exp(ac_x)
        y = y + dskip_ref[:, g * gw:(g + 1) * gw] * xs
        last = ac_x[q - 1:q, :]
        xw = (xdt * jnp.exp(last - ac_x)).astype(BF16)
        s_ref[g] = s_old * jnp.exp(last) + _dot_tn(bm16, xw)
        y = y * zs_ref[:, g * gw:(g + 1) * gw].astype(F32)
        o_ref[:, g * gw:(g + 1) * gw] = (y * _rms_scale(y) * nw_ref[:, g * gw:(g + 1) * gw]).astype(BF16)


def ssd_mixer_core(act, dt_raw, dt_bias, a_log, d_skip, norm_w, *, bsz, seq, inner, heads):
    t = act.shape[0]
    q = SSD_CHUNK
    rows = q * SSD_CHUNKS_PER_STEP
    assert seq % rows == 0
    nc = seq // rows
    conv_dim = inner + 2 * M_GROUPS * M_STATE
    assert conv_dim == 2 * inner and heads <= LANES_V7X
    pad = LANES_V7X - heads
    head_ids = jnp.arange(LANES_V7X)
    n_split = 3
    ehead = (head_ids[:, None] == (jnp.arange(inner) // M_HEAD_DIM)[None, :]).astype(BF16)
    ehead = jnp.concatenate([ehead] * n_split, axis=0)
    z_blk = conv_dim // inner
    row = lambda b, c: b * nc + c
    const = lambda shape: pl.BlockSpec(shape, lambda b, c: (0, 0))
    return pl.pallas_call(
        functools.partial(_ssd_kernel, q=q, groups=M_GROUPS, inner=inner, state=M_STATE),
        grid=(bsz, nc),
        in_specs=[
            pl.BlockSpec((rows, conv_dim), lambda b, c: (row(b, c), 0)),
            pl.BlockSpec((rows, inner), lambda b, c: (row(b, c), z_blk)),
            pl.BlockSpec((rows, LANES_V7X), lambda b, c: (row(b, c), 0)),
            const((1, LANES_V7X)),
            const((1, LANES_V7X)),
            const((1, inner)),
            const((1, inner)),
            const((n_split * LANES_V7X, inner)),
        ],
        out_specs=pl.BlockSpec((rows, inner), lambda b, c: (row(b, c), 0)),
        out_shape=jax.ShapeDtypeStruct((t, inner), BF16),
        scratch_shapes=[pltpu.VMEM((M_GROUPS, M_STATE, inner // M_GROUPS), F32)],
        compiler_params=_cparams(("parallel", "arbitrary")),
        name="ssd_core",
    )(act, act, dt_raw,
      jnp.pad(dt_bias, (0, pad)).reshape(1, LANES_V7X), jnp.pad(a_log, (0, pad)).reshape(1, LANES_V7X),
      jnp.repeat(d_skip, M_HEAD_DIM).reshape(1, inner), norm_w.reshape(1, inner), ehead)


def _hgrn2_kernel(q_ref, f_ref, i_ref, g_ref, lb_ref, nw_ref, o_ref, s_ref, *, q, heads):
    c = pl.program_id(1)
    dk = H_EXPAND

    @pl.when(c == 0)
    def _():
        s_ref[...] = jnp.zeros_like(s_ref)

    lb = lb_ref[...]
    tril = _tril_mask(q)
    tril_sel = _as_sel(tril)
    for sc in range(q_ref.shape[0] // q):
        rows = slice(sc * q, (sc + 1) * q)
        forget = lb + (1.0 - lb) * _sigmoid(f_ref[rows, :])
        gc = _dot_sel_lhs(tril_sel, jnp.log(forget))
        key = 1.0 - forget
        qs = _silu(q_ref[rows, :].astype(F32)) * (dk ** -0.5)
        mid = q // 2 - 1
        g_mid = gc[mid:mid + 1, :]
        g_last = gc[q - 1:q, :]
        q_mid = qs * jnp.exp(gc - g_mid)
        k_mid = key * jnp.exp(g_mid - gc)
        q_dec = q_mid.astype(BF16)
        k_inv = k_mid.astype(BF16)
        q_in = (q_mid * jnp.exp(g_mid)).astype(BF16)
        k_end = (k_mid * jnp.exp(g_last - g_mid)).astype(BF16)
        e_last = jnp.exp(g_last)
        for h in range(heads):
            sl = slice(h * dk, (h + 1) * dk)
            v_h = i_ref[rows, sl]
            att = jnp.where(tril, _dot_nt(q_dec[:, sl], k_inv[:, sl]), 0.0)
            st_old = s_ref[h]
            o_h = _dot(att.astype(BF16), v_h) + _dot_nt(q_in[:, sl], st_old.astype(BF16))
            s_ref[h] = st_old * e_last[:, sl] + _dot_tn(v_h, k_end[:, sl])
            o_h = o_h * _rms_scale(o_h) * nw_ref[...]
            o_ref[rows, sl] = (o_h * _silu(g_ref[rows, sl].astype(F32))).astype(BF16)


def hgrn2_mixer_core(qig, f_raw, lower_bound, norm_w, *, bsz, seq, d):
    t = qig.shape[0]
    q = HGRN_CHUNK
    rows = q * HGRN_CHUNKS_PER_STEP
    assert seq % rows == 0
    nc = seq // rows
    heads = d // H_EXPAND
    row = lambda b, c: b * nc + c
    part = lambda k: pl.BlockSpec((rows, d), lambda b, c: (row(b, c), k))
    return pl.pallas_call(
        functools.partial(_hgrn2_kernel, q=q, heads=heads),
        grid=(bsz, nc),
        in_specs=[part(0), part(0), part(1), part(2),
                  pl.BlockSpec((1, d), lambda b, c: (0, 0)),
                  pl.BlockSpec((1, H_EXPAND), lambda b, c: (0, 0))],
        out_specs=pl.BlockSpec((rows, d), lambda b, c: (row(b, c), 0)),
        out_shape=jax.ShapeDtypeStruct((t, d), BF16),
        scratch_shapes=[pltpu.VMEM((heads, H_EXPAND, d // heads), F32)],
        compiler_params=_cparams(("parallel", "arbitrary")),
        name="hgrn2_core",
    )(qig, f_raw, qig, qig, lower_bound.reshape(1, d), norm_w.reshape(1, H_EXPAND))


def _block_diag(blocks):
    n = len(blocks)
    rows = []
    for e, blk in enumerate(blocks):
        zero = jnp.zeros_like(blk)
        rows.append(jnp.concatenate([blk if k == e else zero for k in range(n)], axis=1))
    return jnp.concatenate(rows, axis=0)


def _gdn_kernel(qkv_ref, zs_ref, ba_ref, dtb_ref, alog_ref, nw_ref, o_ref, s_ref, *, q, qk_heads, v_heads):
    @pl.when(pl.program_id(1) == 0)
    def _():
        s_ref[...] = jnp.zeros_like(s_ref)

    for sc in range(qkv_ref.shape[0] // q):
        rows = pl.ds(sc * q, q)
        _gdn_chunk(qkv_ref.at[rows], zs_ref.at[rows], ba_ref.at[rows], dtb_ref, alog_ref, nw_ref,
                   o_ref.at[rows], s_ref, q=q, qk_heads=qk_heads, v_heads=v_heads)


def _gdn_chunk(qkv_ref, zs_ref, ba_ref, dtb_ref, alog_ref, nw_ref, o_ref, s_ref, *, q, qk_heads, v_heads):
    dh = G_HEAD_DIM
    key_dim = qk_heads * dh
    rep = v_heads // qk_heads
    assert rep == 2 and GDN_QUAD % rep == 0 and 2 * q == LANES_V7X
    tril = _tril_mask(q)
    beta = _sigmoid(ba_ref[:, 0:LANES_V7X])
    gate = -jnp.exp(alog_ref[...]) * _softplus(ba_ref[:, LANES_V7X:2 * LANES_V7X] + dtb_ref[...])
    gc = _dot_sel_lhs(_as_sel(tril), gate)

    eye_t = (lax.broadcasted_iota(jnp.int32, (q, v_heads * q), 0)
             == lax.broadcasted_iota(jnp.int32, (q, v_heads * q), 1) % q)
    col_s = lax.broadcasted_iota(jnp.int32, (q, v_heads * q), 1) % q
    row_l = lax.broadcasted_iota(jnp.int32, (q, v_heads * q), 0)
    lane_lo = lax.broadcasted_iota(jnp.int32, (q, 2 * q), 1) < q
    g_wide = [jnp.broadcast_to(gc[:, h:h + 1], (q, dh)) for h in range(v_heads)]
    b_wide = [jnp.broadcast_to(beta[:, h:h + 1], (q, dh)) for h in range(v_heads)]
    g_col = jnp.concatenate([jnp.where(lane_lo, g_wide[h], g_wide[h + 1]) for h in range(0, v_heads, 2)], axis=1)
    b_col = jnp.concatenate([jnp.where(lane_lo, b_wide[h], b_wide[h + 1]) for h in range(0, v_heads, 2)], axis=1)
    g_row = jnp.sum(jnp.where(eye_t, g_col, 0.0), axis=0, keepdims=True)
    decay = jnp.exp(jnp.where(row_l >= col_s, g_col - g_row, NEG_INF))

    kk_parts, qk_parts, k_heads, k_f32, q_heads = [], [], [], [], []
    for p in range(qk_heads):
        q_p = qkv_ref[:, p * dh:(p + 1) * dh].astype(F32)
        k_p = qkv_ref[:, key_dim + p * dh:key_dim + (p + 1) * dh].astype(F32)
        q_p = q_p * lax.rsqrt(jnp.sum(q_p * q_p, axis=-1, keepdims=True) + EPS) * (dh ** -0.5)
        k_p = k_p * lax.rsqrt(jnp.sum(k_p * k_p, axis=-1, keepdims=True) + EPS)
        k16 = k_p.astype(BF16)
        q16 = q_p.astype(BF16)
        k_rep = jnp.concatenate([k16] * rep, axis=0)
        kk_parts.append(_dot_nt(k16, k_rep))
        qk_parts.append(_dot_nt(q16, k_rep))
        k_heads.append(k16)
        k_f32.append(k_p)
        q_heads.append(q16)
    kk = jnp.concatenate(kk_parts, axis=1)
    qk = jnp.concatenate(qk_parts, axis=1)
    m_all = jnp.where(row_l > col_s, b_col * kk * decay, 0.0)
    att_all = qk * decay

    quad_w = GDN_QUAD * q
    n_quads = v_heads // GDN_QUAD
    lane_blk = lax.broadcasted_iota(jnp.int32, (q, quad_w), 1) // q
    blk_sel = [jnp.where(lane_blk == e, 1.0, 0.0).astype(BF16) for e in range(GDN_QUAD)]

    def blocks_on_diagonal(x16):
        return jnp.concatenate([x16 * blk_sel[e] for e in range(GDN_QUAD)], axis=0)

    def times_blockwise(lhs_hi, lhs_lo, bd_hi, bd_lo):
        lhs = jnp.concatenate([jnp.concatenate([hi, lo, hi], axis=1) for hi, lo in zip(lhs_hi, lhs_lo)], axis=0)
        out = _dot(lhs, jnp.concatenate([bd_hi, bd_hi, bd_lo], axis=0))
        return [out[i * q:(i + 1) * q, :] for i in range(len(lhs_hi))]

    levels = q.bit_length() - 2
    p_cur = [-m_all[:, u * quad_w:(u + 1) * quad_w] for u in range(n_quads)]
    n_acc = list(p_cur)
    for lvl in range(levels + 1):
        for u in range(n_quads):
            p_hi, p_lo = _split2(p_cur[u])
            bd_hi, bd_lo = blocks_on_diagonal(p_hi), blocks_on_diagonal(p_lo)
            if lvl == 0:
                (p_cur[u],) = times_blockwise([p_hi], [p_lo], bd_hi, bd_lo)
            elif lvl < levels:
                n_hi, n_lo = _split2(n_acc[u])
                n_p, p_sq = times_blockwise([n_hi, p_hi], [n_lo, p_lo], bd_hi, bd_lo)
                n_acc[u] = n_acc[u] + p_cur[u] + n_p
                p_cur[u] = p_sq
            else:
                n_hi, n_lo = _split2(n_acc[u])
                (n_p,) = times_blockwise([n_hi], [n_lo], bd_hi, bd_lo)
                n_acc[u] = n_acc[u] + p_cur[u] + n_p

    def two_pass(a, rhs16):
        a_hi, a_lo = _split2(a)
        return _dot(jnp.concatenate([a_hi, a_lo], axis=1), jnp.concatenate([rhs16, rhs16], axis=0))

    b_row = jnp.sum(jnp.where(eye_t, b_col, 0.0), axis=0, keepdims=True)
    bg_row = b_row * jnp.exp(g_row)
    eye_f = jnp.where(eye_t, 1.0, 0.0)
    zero_blk = jnp.zeros((dh, dh), BF16)
    pairs_per_quad = GDN_QUAD // rep
    for u in range(n_quads):
        cols = slice(u * quad_w, (u + 1) * quad_w)
        t_b = (n_acc[u] + eye_f[:, cols]) * b_row[:, cols]
        t_bg = (n_acc[u] + eye_f[:, cols]) * bg_row[:, cols]
        v16 = [qkv_ref[:, 2 * key_dim + h * dh:2 * key_dim + (h + 1) * dh]
               for h in range(u * GDN_QUAD, (u + 1) * GDN_QUAD)]
        k16 = [k_heads[h // rep] for h in range(u * GDN_QUAD, (u + 1) * GDN_QUAD)]
        u_quad = two_pass(t_b, _block_diag(v16))
        w_quad = two_pass(t_bg, _block_diag(k16))
        v_new, o_inter = [], []
        for pi in range(pairs_per_quad):
            p = u * pairs_per_quad + pi
            pc = slice(pi * rep * dh, (pi + 1) * rep * dh)
            g_pair = jnp.concatenate(g_wide[p * rep:(p + 1) * rep], axis=1)
            g_last = g_pair[q - 1:q, :]
            s_old = s_ref[p]
            s16 = s_old.astype(BF16)
            s_bd = jnp.concatenate([jnp.concatenate([s16[:, 0:dh], zero_blk], axis=1),
                                    jnp.concatenate([zero_blk, s16[:, dh:2 * dh]], axis=1)], axis=0)
            lhs = jnp.concatenate([w_quad[:, pc].astype(BF16), jnp.concatenate([q_heads[p]] * rep, axis=1)],
                                  axis=0)
            both = _dot(lhs, s_bd)
            vn = u_quad[:, pc] - both[0:q, :]
            o_inter.append(both[q:2 * q, :] * jnp.exp(g_pair))
            vn16 = vn.astype(BF16)
            zero_q = jnp.zeros((q, dh), BF16)
            vn_bd = jnp.concatenate([jnp.concatenate([vn16[:, 0:dh], zero_q], axis=1),
                                     jnp.concatenate([zero_q, vn16[:, dh:2 * dh]], axis=1)], axis=0)
            k_dec = jnp.concatenate([k_f32[p]] * rep, axis=1) * jnp.exp(g_last - g_pair)
            ke = jnp.concatenate([k_dec[:, 0:dh], k_dec[:, dh:2 * dh]], axis=0).astype(BF16)
            s_ref[p] = s_old * jnp.exp(g_last) + _dot_tn(ke, vn_bd)
            v_new.extend([vn16[:, 0:dh], vn16[:, dh:2 * dh]])
        o_intra = _dot(att_all[:, cols].astype(BF16), _block_diag(v_new))
        o_quad = jnp.concatenate(o_inter, axis=1) + o_intra
        for e in range(GDN_QUAD):
            h = u * GDN_QUAD + e
            sl = slice(h * dh, (h + 1) * dh)
            o_h = o_quad[:, e * dh:(e + 1) * dh]
            o_h = o_h * _rms_scale(o_h) * nw_ref[...]
            o_ref[:, sl] = (o_h * zs_ref[:, sl].astype(F32)).astype(BF16)


def gdn_mixer_core(act, ba_raw, a_log, dt_bias, norm_w, *, bsz, seq, qk_heads, v_heads):
    t = act.shape[0]
    q = GDN_CHUNK
    rows = q * GDN_CHUNKS_PER_STEP
    assert seq % rows == 0
    nc = seq // rows
    dh = G_HEAD_DIM
    key_dim = qk_heads * dh
    val_dim = v_heads * dh
    conv_dim = 2 * key_dim + val_dim
    assert conv_dim == 2 * val_dim and v_heads % GDN_QUAD == 0
    pad = LANES_V7X - v_heads
    z_blk = conv_dim // val_dim
    row = lambda b, c: b * nc + c
    const = lambda shape: pl.BlockSpec(shape, lambda b, c: (0, 0))
    return pl.pallas_call(
        functools.partial(_gdn_kernel, q=q, qk_heads=qk_heads, v_heads=v_heads),
        grid=(bsz, nc),
        in_specs=[
            pl.BlockSpec((rows, conv_dim), lambda b, c: (row(b, c), 0)),
            pl.BlockSpec((rows, val_dim), lambda b, c: (row(b, c), z_blk)),
            pl.BlockSpec((rows, 2 * LANES_V7X), lambda b, c: (row(b, c), 0)),
            const((1, LANES_V7X)),
            const((1, LANES_V7X)),
            const((1, dh)),
        ],
        out_specs=pl.BlockSpec((rows, val_dim), lambda b, c: (row(b, c), 0)),
        out_shape=jax.ShapeDtypeStruct((t, val_dim), BF16),
        scratch_shapes=[pltpu.VMEM((qk_heads, dh, (v_heads // qk_heads) * dh), F32)],
        compiler_params=_cparams(("parallel", "arbitrary")),
        name="gdn_core",
    )(act, act, ba_raw, jnp.pad(dt_bias, (0, pad)).reshape(1, LANES_V7X),
      jnp.pad(a_log, (0, pad)).reshape(1, LANES_V7X), norm_w.reshape(1, dh))


def _pad_cols(w, width):
    return jnp.pad(w, ((0, 0), (0, width - w.shape[1])))


def _ssd_in_weights(in_w, inner, conv_dim):
    z = in_w[:, :inner]
    xbc = in_w[:, inner:inner + conv_dim]
    dt = _pad_cols(in_w[:, inner + conv_dim:], LANES_V7X)
    return jnp.concatenate([xbc, z], axis=1).astype(BF16), dt.astype(BF16)


def _gdn_in_weights(in_w, conv_dim, val_dim, v_heads):
    main = in_w[:, :conv_dim + val_dim]
    b = _pad_cols(in_w[:, conv_dim + val_dim:conv_dim + val_dim + v_heads], LANES_V7X)
    a = _pad_cols(in_w[:, conv_dim + val_dim + v_heads:], LANES_V7X)
    return main.astype(BF16), jnp.concatenate([b, a], axis=1).astype(BF16)


def _pick_tile(n, candidates):
    for cand in candidates:
        if n % cand == 0:
            return cand
    raise ValueError(f"no tile for {n} in {candidates}")


def kernel(x, mem, ln_mix, ln_xattn, ln_mem, ln_ffn, final_norm, m_in_w, m_conv_w, m_conv_b, m_dt_bias, m_a_log, m_d, m_norm_w, m_out_w, h_in_w, h_lower_bounds, h_norm_w, h_out_w, g_in_w, g_conv_w, g_a_log, g_dt_bias, g_norm_w, g_out_w, xa_q, xa_kv, xa_o, f_up, f_conv_w, f_conv_b, f_down):
    bsz, seq, d = x.shape
    n_mem = mem.shape[1]
    depth = ln_mix.shape[0]
    t = bsz * seq
    tm = _pick_tile(t, (1024, 512, 256, 128, 64))
    tq = _pick_tile(seq, (512, 256, 128, 64))
    ts = _pick_tile(seq, (1024, 512, 256, 128, 64))

    lb = jnp.cumsum(jax.nn.softmax(h_lower_bounds.astype(F32), axis=0), axis=0)
    lb = lb - lb[:1]

    m_inner = m_out_w.shape[1]
    m_heads = m_dt_bias.shape[1]
    m_conv_dim = m_conv_w.shape[2]
    g_v_heads = g_a_log.shape[1]
    g_val_dim = g_out_w.shape[1]
    g_conv_dim = g_conv_w.shape[2]
    g_qk_heads = (g_conv_dim - g_val_dim) // (2 * G_HEAD_DIM)

    xa_q16, xa_kv16, xa_o16 = (w.astype(BF16) for w in (xa_q, xa_kv, xa_o))
    h_in16 = jnp.concatenate([h_in_w[..., :d], h_in_w[..., 2 * d:], h_in_w[..., d:2 * d]], axis=-1).astype(BF16)
    f_up16, f_down16 = f_up.astype(BF16), f_down.astype(BF16)
    m_out16, h_out16, g_out16 = m_out_w.astype(BF16), h_out_w.astype(BF16), g_out_w.astype(BF16)

    xf = x.reshape(t, d)
    memf = mem.reshape(bsz * n_mem, d)
    ia = ib = ic = 0
    for i in range(depth):
        kind = i % N_MIXERS
        if kind == 0:
            w_in, w_dt = _ssd_in_weights(m_in_w[ia], m_inner, m_conv_dim)
            act, dt_raw = mixer_in_proj(xf, ln_mix[i], w_in, w_dt, m_conv_w[ia], m_conv_b[ia],
                                        bsz=bsz, seq=seq, tm=ts, tn=1024)
            y = ssd_mixer_core(act, dt_raw, m_dt_bias[ia], m_a_log[ia], m_d[ia],
                               m_norm_w[ia], bsz=bsz, seq=seq, inner=m_inner, heads=m_heads)
            w_out, out_layer = m_out16, ia
            ia += 1
        elif kind == 1:
            qig, f_raw = norm_matmul_split(xf, ln_mix[i], h_in16, n16=3 * d, layer=ib, tm=tm, tn=1024)
            y = hgrn2_mixer_core(qig, f_raw, lb[i], h_norm_w[ib], bsz=bsz, seq=seq, d=d)
            w_out, out_layer = h_out16, ib
            ib += 1
        else:
            w_in, w_ba = _gdn_in_weights(g_in_w[ic], g_conv_dim, g_val_dim, g_v_heads)
            act, ba_raw = mixer_in_proj(xf, ln_mix[i], w_in, w_ba, g_conv_w[ic], jnp.zeros((g_conv_dim,), F32),
                                        bsz=bsz, seq=seq, tm=ts, tn=1024)
            y = gdn_mixer_core(act, ba_raw, g_a_log[ic], g_dt_bias[ic], g_norm_w[ic],
                               bsz=bsz, seq=seq, qk_heads=g_qk_heads, v_heads=g_v_heads)
            w_out, out_layer = g_out16, ic
            ic += 1

        kv = norm_matmul(memf, ln_mem[i], xa_kv16, layer=i, tm=_pick_tile(bsz * n_mem, (1024, 512, 256)),
                         tn=1024, out_dtype=BF16)
        xf = outproj_cross_attention(xf, y, w_out, out_layer, ln_xattn[i], xa_q16, kv, xa_o16, layer=i,
                                     bsz=bsz, seq=seq, n_mem=n_mem, tq=ts)

        last = i == depth - 1
        xf = conv_glu_ffn(xf, ln_ffn[i], f_up16, f_conv_w[i], f_conv_b[i], f_down16, layer=i,
                          bsz=bsz, seq=seq, tm=tq, final_gain=final_norm if last else None)
    return xf.reshape(bsz, seq, d)
```
